```python
import jax, jax.numpy as jnp
from jax import lax
import numpy as np

D_MODEL = 1024
BATCH = 4
SEQ = 8192
DEPTH = 2

N_ATTN_HEADS = 8
ATTN_HEAD_DIM = 64
ATTN_WIDTH = N_ATTN_HEADS * ATTN_HEAD_DIM
N_REC_HEADS = 4
REC_HEAD_DIM = 128
REC_WIDTH = N_REC_HEADS * REC_HEAD_DIM
D_MIX = ATTN_WIDTH + REC_WIDTH
IN_COLS = 3 * ATTN_WIDTH + N_ATTN_HEADS + 4 * REC_WIDTH
SPLIT_POINTS = (ATTN_WIDTH, 2 * ATTN_WIDTH, 3 * ATTN_WIDTH,
                3 * ATTN_WIDTH + N_ATTN_HEADS,
                3 * ATTN_WIDTH + N_ATTN_HEADS + REC_WIDTH,
                3 * ATTN_WIDTH + N_ATTN_HEADS + 2 * REC_WIDTH,
                3 * ATTN_WIDTH + N_ATTN_HEADS + 3 * REC_WIDTH)
D_FF = 2816
CONV_WIDTH = 3
FOX_BLOCK = 128
HGRN_CHUNK = 64
LN_EPS = 1e-5
RMS_EPS = 1e-6
DEEPNORM_ALPHA = (2.0 * DEPTH) ** 0.25
DEEPNORM_BETA = (8.0 * DEPTH) ** -0.25

kernel_name = 'hybrid_fox_hgrn2_deepnorm'


def _layer_norm(x, g, b):
    xf = x.astype(jnp.float32)
    mu = jnp.mean(xf, axis=-1, keepdims=True)
    var = jnp.mean(jnp.square(xf - mu), axis=-1, keepdims=True)
    y = (xf - mu) * lax.rsqrt(var + LN_EPS)
    return (y * g.astype(jnp.float32) + b.astype(jnp.float32)).astype(x.dtype)


def _rms_norm_heads(x, g):
    xf = x.astype(jnp.float32)
    y = xf * lax.rsqrt(jnp.mean(jnp.square(xf), axis=-1, keepdims=True) + RMS_EPS)
    return y * g.astype(jnp.float32)


def _fox_attention(q, k, v, logf):
    B, H, S, D = q.shape
    n_blocks = S // FOX_BLOCK
    c = jnp.cumsum(logf.astype(jnp.float32), axis=-1)
    qf = q.astype(jnp.float32) * (D ** -0.5)
    kf = k.astype(jnp.float32)
    vf = v.astype(jnp.float32)
    q_blocks = qf.reshape(B, H, n_blocks, FOX_BLOCK, D).transpose(2, 0, 1, 3, 4)
    c_blocks = c.reshape(B, H, n_blocks, FOX_BLOCK).transpose(2, 0, 1, 3)
    key_pos = jnp.arange(S)

    def block(args):
        q_i, c_i, i = args
        s = jnp.einsum('bhqd,bhkd->bhqk', q_i, kf) + c_i[..., :, None] - c[..., None, :]
        q_pos = i * FOX_BLOCK + jnp.arange(FOX_BLOCK)
        s = jnp.where(key_pos[None, :] <= q_pos[:, None], s, -jnp.inf)
        p = jax.nn.softmax(s, axis=-1)
        return jnp.einsum('bhqk,bhkd->bhqd', p, vf)

    o = lax.map(block, (q_blocks, c_blocks, jnp.arange(n_blocks)))
    return o.transpose(1, 2, 0, 3, 4).reshape(B, H, S, D)


def _hgrn2_chunked(q, k, v, logf):
    B, H, S, DK = q.shape
    DV = v.shape[-1]
    C = HGRN_CHUNK
    n_chunks = S // C

    def to_chunks(t):
        return t.astype(jnp.float32).reshape(B, H, n_chunks, C, t.shape[-1]).transpose(2, 0, 1, 3, 4)

    qc, kc, vc, gc = to_chunks(q), to_chunks(k), to_chunks(v), to_chunks(logf)
    b = jnp.cumsum(gc, axis=-2)
    causal = jnp.tril(jnp.ones((C, C), dtype=bool))

    def step(state, inp):
        q_i, k_i, v_i, b_i = inp
        diff = b_i[..., :, None, :] - b_i[..., None, :, :]
        decay = jnp.where(causal[:, :, None], jnp.exp(jnp.minimum(diff, 0.0)), 0.0)
        scores = jnp.einsum('bhtd,bhsd,bhtsd->bhts', q_i, k_i, decay)
        o = jnp.einsum('bhts,bhse->bhte', scores, v_i) \
            + jnp.einsum('bhtd,bhde->bhte', q_i * jnp.exp(b_i), state)
        b_last = b_i[..., -1:, :]
        state = jnp.exp(b_last[..., 0, :])[..., None] * state \
            + jnp.einsum('bhsd,bhse->bhde', k_i * jnp.exp(b_last - b_i), v_i)
        return state, o

    state0 = jnp.zeros((B, H, DK, DV), jnp.float32)
    _, o = lax.scan(step, state0, (qc, kc, vc, b))
    return o.transpose(1, 2, 0, 3, 4).reshape(B, H, S, DV)


def _causal_depthwise_conv(h, w, bias):
    S = h.shape[1]
    hp = jnp.pad(h, ((0, 0), (CONV_WIDTH - 1, 0), (0, 0)))
    y = bias
    for j in range(CONV_WIDTH):
        y = y + w[j] * hp[:, j:j + S, :]
    return y


def setup_inputs(seed: int = 0) -> dict:
    key = jax.random.key(seed)
    ks = jax.random.split(key, 17)

    def nrm(k, shape, scale):
        return jax.random.normal(k, shape, jnp.float32) * scale

    col_scale = jnp.concatenate([
        jnp.ones((2 * ATTN_WIDTH,), jnp.float32),
        jnp.full((ATTN_WIDTH,), DEEPNORM_BETA, jnp.float32),
        jnp.ones((N_ATTN_HEADS + 2 * REC_WIDTH,), jnp.float32),
        jnp.full((REC_WIDTH,), DEEPNORM_BETA, jnp.float32),
        jnp.ones((REC_WIDTH,), jnp.float32)])

    return {
        'x': nrm(ks[0], (BATCH, SEQ, D_MODEL), 1.0),
        'ln_emb_g': 1.0 + nrm(ks[1], (D_MODEL,), 0.05),
        'ln_emb_b': nrm(ks[2], (D_MODEL,), 0.02),
        'w_in': nrm(ks[3], (DEPTH, D_MODEL, IN_COLS), D_MODEL ** -0.5) * col_scale,
        'fox_f_bias': 1.0 + nrm(ks[4], (DEPTH, N_ATTN_HEADS), 0.1),
        'fox_norm_g': 1.0 + nrm(ks[5], (DEPTH, ATTN_WIDTH), 0.05),
        'hgrn_lower_bounds': nrm(ks[6], (DEPTH, REC_WIDTH), 0.1),
        'hgrn_norm_g': 1.0 + nrm(ks[7], (DEPTH, REC_WIDTH), 0.05),
        'w_o': nrm(ks[8], (DEPTH, D_MIX, D_MODEL), D_MIX ** -0.5 * DEEPNORM_BETA),
        'ln_mix_g': 1.0 + nrm(ks[9], (DEPTH, D_MODEL), 0.05),
        'ln_mix_b': nrm(ks[10], (DEPTH, D_MODEL), 0.02),
        'w_up': nrm(ks[11], (DEPTH, D_MODEL, 2 * D_FF), D_MODEL ** -0.5 * DEEPNORM_BETA),
        'conv_w': nrm(ks[12], (DEPTH, CONV_WIDTH, 2 * D_FF), CONV_WIDTH ** -0.5),
        'conv_b': nrm(ks[13], (DEPTH, 2 * D_FF), 0.02),
        'w_down': nrm(ks[14], (DEPTH, D_FF, D_MODEL), D_FF ** -0.5 * DEEPNORM_BETA),
        'ln_ffn_g': 1.0 + nrm(ks[15], (DEPTH, D_MODEL), 0.05),
        'ln_ffn_b': nrm(ks[16], (DEPTH, D_MODEL), 0.02),
    }


def reference(x, ln_emb_g, ln_emb_b, w_in, fox_f_bias, fox_norm_g, hgrn_lower_bounds,
              hgrn_norm_g, w_o, ln_mix_g, ln_mix_b, w_up, conv_w, conv_b, w_down,
              ln_ffn_g, ln_ffn_b):
    B, S, _ = x.shape
    dtype = x.dtype

    lb_sm = jax.nn.softmax(hgrn_lower_bounds.astype(jnp.float32), axis=0)
    lb_cum = jnp.cumsum(lb_sm, axis=0)
    lower_bounds = lb_cum - lb_cum[0:1]

    def heads(t, n_heads):
        return t.reshape(B, S, n_heads, -1).transpose(0, 2, 1, 3)

    x = _layer_norm(x, ln_emb_g, ln_emb_b)

    for l in range(DEPTH):
        proj = x @ w_in[l]
        q_a, k_a, v_a, f_a, q_r, f_r, i_r, g_r = jnp.split(proj, SPLIT_POINTS, axis=-1)

        logf_a = jax.nn.log_sigmoid((f_a + fox_f_bias[l]).astype(jnp.float32)).transpose(0, 2, 1)
        o_a = _fox_attention(heads(q_a, N_ATTN_HEADS), heads(k_a, N_ATTN_HEADS),
                             heads(v_a, N_ATTN_HEADS), logf_a)
        o_a = _rms_norm_heads(o_a.transpose(0, 2, 1, 3),
                              fox_norm_g[l].reshape(N_ATTN_HEADS, ATTN_HEAD_DIM))
        o_a = o_a.reshape(B, S, ATTN_WIDTH)

        lb = lower_bounds[l]
        logf_r = jnp.logaddexp(jnp.log(lb),
                               jnp.log1p(-lb) + jax.nn.log_sigmoid(f_r.astype(jnp.float32)))
        k_r = -jnp.expm1(logf_r)
        o_r = _hgrn2_chunked(heads(jax.nn.silu(q_r), N_REC_HEADS), heads(k_r, N_REC_HEADS),
                             heads(i_r, N_REC_HEADS), heads(logf_r, N_REC_HEADS))
        o_r = _rms_norm_heads(o_r.transpose(0, 2, 1, 3),
                              hgrn_norm_g[l].reshape(N_REC_HEADS, REC_HEAD_DIM))
        o_r = o_r.reshape(B, S, REC_WIDTH) * jax.nn.silu(g_r.astype(jnp.float32))

        mix = jnp.concatenate([o_a, o_r], axis=-1).astype(dtype) @ w_o[l]
        x = _layer_norm(DEEPNORM_ALPHA * x + mix, ln_mix_g[l], ln_mix_b[l])

        h = _causal_depthwise_conv(x @ w_up[l], conv_w[l], conv_b[l])
        a, u = jnp.split(h, 2, axis=-1)
        ffn = (jax.nn.gelu(a, approximate=False) * u) @ w_down[l]
        x = _layer_norm(DEEPNORM_ALPHA * x + ffn, ln_ffn_g[l], ln_ffn_b[l])

    return x
```

```python
import functools
import math

import jax
import jax.numpy as jnp
from jax import lax
from jax.experimental import pallas as pl
from jax.experimental.pallas import tpu as pltpu

F32 = jnp.float32
BF16 = jnp.bfloat16

N_ATTN_HEADS = 8
ATTN_HEAD_DIM = 64
ATTN_WIDTH = N_ATTN_HEADS * ATTN_HEAD_DIM
N_REC_HEADS = 4
REC_HEAD_DIM = 128
REC_WIDTH = N_REC_HEADS * REC_HEAD_DIM
CONV_WIDTH = 3
LN_EPS = 1e-5
RMS_EPS = 1e-6

LANES = 128
SUBLANES = 8
VMEM_LIMIT_BYTES = 56 * 1024 * 1024

PROJ_TOKENS = 512
ATTN_TILE = 512
REC_TOKENS = 512
REC_CHUNK = 64
OUT_TOKENS = 512
FFN_TOKENS = 512
FFN_BLOCK = 256
HALO = SUBLANES


def _resident(shape):
    nd = len(shape)
    return pl.BlockSpec(shape, lambda *_: (0,) * nd, pipeline_mode=pl.Buffered(1))


def _layer_norm(x, g, b):
    mu = jnp.mean(x, axis=-1, keepdims=True)
    xc = x - mu
    var = jnp.mean(xc * xc, axis=-1, keepdims=True)
    return xc * lax.rsqrt(var + LN_EPS) * g + b


def _log_sigmoid(z):
    return jnp.minimum(z, 0.0) - jnp.log(1.0 + jnp.exp(-jnp.abs(z)))


def _sigmoid(z):
    return 1.0 / (1.0 + jnp.exp(-z))


def _silu(z):
    return z * _sigmoid(z)


def _proj_kernel(*refs, apply_ln, layer, tm):
    if apply_ln:
        x_ref, lng_ref, lnb_ref = refs[:3]
        refs = refs[3:]
    else:
        x_ref = refs[0]
        refs = refs[1:]
    (wqkv_ref, wfat_ref, fbias_ref, wr_ref, lbraw_ref) = refs[:5]
    refs = refs[5:]
    if apply_ln:
        xln_ref = refs[0]
        refs = refs[1:]
    (q_ref, k_ref, v_ref, ct_ref, qr_ref, kr_ref, ir_ref, gr_ref, lfr_ref, carry_ref) = refs

    s_idx = pl.program_id(1)
    x = x_ref[0]
    if apply_ln:
        x = _layer_norm(x, lng_ref[...], lnb_ref[...])
        xln_ref[0] = x
    xb = x.astype(BF16)

    qkv = jnp.dot(xb, wqkv_ref[...], preferred_element_type=F32)
    q_ref[0] = (qkv[:, :ATTN_WIDTH] * (ATTN_HEAD_DIM ** -0.5)).astype(BF16)
    k_ref[0] = qkv[:, ATTN_WIDTH:2 * ATTN_WIDTH].astype(BF16)
    v_ref[0] = qkv[:, 2 * ATTN_WIDTH:].astype(BF16)

    ft = lax.dot_general(wfat_ref[...], xb, (((1,), (1,)), ((), ())), preferred_element_type=F32)
    logf = _log_sigmoid(ft[:N_ATTN_HEADS] + fbias_ref[...])
    lane = lax.broadcasted_iota(jnp.int32, logf.shape, 1)
    c = logf
    shift = 1
    while shift < tm:
        c = c + jnp.where(lane >= shift, pltpu.roll(c, shift, 1), 0.0)
        shift *= 2

    @pl.when(s_idx == 0)
    def _():
        carry_ref[...] = jnp.zeros_like(carry_ref)

    c = c + carry_ref[:, 0:1]
    carry_ref[...] = jnp.broadcast_to(c[:, tm - 1:tm], carry_ref.shape)
    row = lax.broadcasted_iota(jnp.int32, c.shape, 0)
    for p in range(N_ATTN_HEADS // 2):
        rolled = c if p == 0 else pltpu.roll(c, SUBLANES - 2 * p, 0)
        ct_ref[0, p] = jnp.where(row < 2, rolled, 0.0)

    r = jnp.dot(xb, wr_ref[...], preferred_element_type=F32)
    qr_ref[0] = _silu(r[:, :REC_WIDTH]).astype(BF16)
    f_r = r[:, REC_WIDTH:2 * REC_WIDTH]
    ls = _log_sigmoid(f_r)
    if layer == 0:
        lfr_ref[0] = ls
        kr_ref[0] = _sigmoid(-f_r).astype(BF16)
    else:
        raw = lbraw_ref[...]
        e = jnp.exp(raw - jnp.max(raw, axis=0, keepdims=True))
        sm = e / jnp.sum(e, axis=0, keepdims=True)
        lb = jnp.sum(sm[1:layer + 1], axis=0, keepdims=True)
        a = jnp.log(lb)
        b = jnp.log(1.0 - lb) + ls
        hi = jnp.maximum(a, b)
        lfr_ref[0] = hi + jnp.log(1.0 + jnp.exp(-jnp.abs(a - b)))
        kr_ref[0] = ((1.0 - lb) * _sigmoid(-f_r)).astype(BF16)
    ir_ref[0] = r[:, 2 * REC_WIDTH:3 * REC_WIDTH].astype(BF16)
    gr_ref[0] = _silu(r[:, 3 * REC_WIDTH:]).astype(BF16)


def _project(x, ln_g, ln_b, wqkv, wfat, fbias, wr, lbraw, *, layer):
    B, S, D = x.shape
    tm = PROJ_TOKENS
    apply_ln = ln_g is not None
    depth = lbraw.shape[0]
    tok = lambda w: pl.BlockSpec((1, tm, w), lambda b, s: (b, s, 0))
    in_specs = [tok(D)]
    args = [x]
    if apply_ln:
        in_specs += [_resident((1, D)), _resident((1, D))]
        args += [ln_g, ln_b]
    in_specs += [_resident(wqkv.shape), _resident(wfat.shape), _resident(fbias.shape),
                 _resident(wr.shape), _resident((depth, REC_WIDTH))]
    args += [wqkv, wfat, fbias, wr, lbraw]
    out_shape, out_specs = [], []
    if apply_ln:
        out_shape.append(jax.ShapeDtypeStruct((B, S, D), F32))
        out_specs.append(tok(D))
    for _ in range(3):
        out_shape.append(jax.ShapeDtypeStruct((B, S, ATTN_WIDTH), BF16))
        out_specs.append(tok(ATTN_WIDTH))
    out_shape.append(jax.ShapeDtypeStruct((B, N_ATTN_HEADS // 2, SUBLANES, S), F32))
    out_specs.append(pl.BlockSpec((1, N_ATTN_HEADS // 2, SUBLANES, tm), lambda b, s: (b, 0, 0, s)))
    for _ in range(4):
        out_shape.append(jax.ShapeDtypeStruct((B, S, REC_WIDTH), BF16))
        out_specs.append(tok(REC_WIDTH))
    out_shape.append(jax.ShapeDtypeStruct((B, S, REC_WIDTH), F32))
    out_specs.append(tok(REC_WIDTH))
    outs = pl.pallas_call(
        functools.partial(_proj_kernel, apply_ln=apply_ln, layer=layer, tm=tm),
        grid=(B, S // tm),
        in_specs=in_specs,
        out_specs=out_specs,
        out_shape=out_shape,
        scratch_shapes=[pltpu.VMEM((SUBLANES, LANES), F32)],
        compiler_params=pltpu.CompilerParams(
            dimension_semantics=("arbitrary", "arbitrary"), vmem_limit_bytes=VMEM_LIMIT_BYTES),
        name=f"proj_l{layer}",
    )(*args)
    return outs


def _fox_kernel(q_ref, k_ref, v_ref, cq_ref, ck_ref, g_ref, o_ref, m_ref, l_ref, acc_ref, *, t):
    qi = pl.program_id(2)
    q = q_ref[0]
    lane = lax.broadcasted_iota(jnp.int32, q.shape, 1)
    first = lane < ATTN_HEAD_DIM
    zero = jnp.zeros_like(q)
    q_heads = (jnp.where(first, q, zero), jnp.where(first, zero, q))
    c_q = jnp.transpose(cq_ref[0, 0])
    c_rows = (c_q[:, 0:1], c_q[:, 1:2])

    m_ref[...] = jnp.full_like(m_ref, -jnp.inf)
    l_ref[...] = jnp.zeros_like(l_ref)
    acc_ref[...] = jnp.zeros_like(acc_ref)

    def tile(kj, masked):
        start = pl.multiple_of(kj * t, t)
        k = k_ref[0, pl.ds(start, t), :]
        v = v_ref[0, pl.ds(start, t), :]
        for h in range(2):
            c_k = ck_ref[0, 0, h:h + 1, pl.ds(start, t)]
            s = lax.dot_general(q_heads[h], k, (((1,), (1,)), ((), ())), preferred_element_type=F32)
            s = s + (c_rows[h] - c_k)
            if masked:
                r_i = lax.broadcasted_iota(jnp.int32, s.shape, 0)
                c_i = lax.broadcasted_iota(jnp.int32, s.shape, 1)
                s = jnp.where(c_i <= r_i, s, -jnp.inf)
            m_old = m_ref[h]
            m_new = jnp.maximum(m_old, jnp.max(s, axis=-1, keepdims=True))
            alpha = jnp.exp(m_old - m_new)
            p = jnp.exp(s - m_new)
            l_ref[h] = alpha * l_ref[h] + jnp.sum(p, axis=-1, keepdims=True)
            acc_ref[h] = alpha * acc_ref[h] + jnp.dot(p.astype(BF16), v, preferred_element_type=F32)
            m_ref[h] = m_new

    def body(kj, carry):
        tile(kj, False)
        return carry

    lax.fori_loop(0, qi, body, 0)
    tile(qi, True)

    lane_o = lax.broadcasted_iota(jnp.int32, (t, LANES), 1)
    first_o = lane_o < ATTN_HEAD_DIM
    o = jnp.where(first_o, acc_ref[0] / l_ref[0], acc_ref[1] / l_ref[1])
    sq = o * o
    ms0 = jnp.sum(jnp.where(first_o, sq, 0.0), axis=-1, keepdims=True) / ATTN_HEAD_DIM
    ms1 = jnp.sum(jnp.where(first_o, 0.0, sq), axis=-1, keepdims=True) / ATTN_HEAD_DIM
    scale = jnp.where(first_o, lax.rsqrt(ms0 + RMS_EPS), lax.rsqrt(ms1 + RMS_EPS))
    o_ref[0] = (o * scale * g_ref[...]).astype(o_ref.dtype)


def _fox_attention(q, k, v, ct, norm_g, *, layer):
    B, S, _ = q.shape
    t = ATTN_TILE
    pairs = N_ATTN_HEADS // 2
    return pl.pallas_call(
        functools.partial(_fox_kernel, t=t),
        grid=(B, pairs, S // t),
        in_specs=[
            pl.BlockSpec((1, t, LANES), lambda b, p, i: (b, i, p)),
            pl.BlockSpec((1, S, LANES), lambda b, p, i: (b, 0, p)),
            pl.BlockSpec((1, S, LANES), lambda b, p, i: (b, 0, p)),
            pl.BlockSpec((1, 1, SUBLANES, t), lambda b, p, i: (b, p, 0, i)),
            pl.BlockSpec((1, 1, SUBLANES, S), lambda b, p, i: (b, p, 0, 0)),
            pl.BlockSpec((1, LANES), lambda b, p, i: (0, p)),
        ],
        out_specs=pl.BlockSpec((1, t, LANES), lambda b, p, i: (b, i, p)),
        out_shape=jax.ShapeDtypeStruct((B, S, ATTN_WIDTH), BF16),
        scratch_shapes=[pltpu.VMEM((2, t, 1), F32), pltpu.VMEM((2, t, 1), F32),
                        pltpu.VMEM((2, t, LANES), F32)],
        compiler_params=pltpu.CompilerParams(
            dimension_semantics=("arbitrary", "arbitrary", "arbitrary"),
            vmem_limit_bytes=VMEM_LIMIT_BYTES),
        name=f"fox_l{layer}",
    )(q, k, v, ct, ct, norm_g)


def _reference_rows(b3, level):
    nc, cs, d = b3.shape
    if level >= SUBLANES:
        parts = []
        for g0 in range(0, cs, 2 * level):
            row = b3[:, g0 + level:g0 + level + 1, :]
            parts.append(jnp.broadcast_to(row, (nc, 2 * level, d)))
        return jnp.concatenate(parts, axis=1)
    b4 = b3.reshape(nc * cs // SUBLANES, SUBLANES, d)
    sub = lax.broadcasted_iota(jnp.int32, b4.shape, 1)
    ref = jnp.broadcast_to(b4[:, SUBLANES - level:SUBLANES - level + 1, :], b4.shape)
    for g0 in range(SUBLANES - 4 * level, -1, -2 * level):
        row = jnp.broadcast_to(b4[:, g0 + level:g0 + level + 1, :], b4.shape)
        ref = jnp.where(sub < g0 + 2 * level, row, ref)
    return ref.reshape(nc, cs, d)


def _hgrn_kernel(q_ref, k_ref, v_ref, lf_ref, g_ref, ng_ref, o_ref, st_ref, *, ts):
    cs = REC_CHUNK
    nc = ts // cs
    d = REC_HEAD_DIM

    @pl.when(pl.program_id(1) == 0)
    def _():
        st_ref[...] = jnp.zeros_like(st_ref)

    row2 = lax.broadcasted_iota(jnp.int32, (ts, d), 0) % cs
    row3 = lax.broadcasted_iota(jnp.int32, (nc, cs, d), 1)
    t_i = lax.broadcasted_iota(jnp.int32, (nc, cs, cs), 1)
    s_i = lax.broadcasted_iota(jnp.int32, (nc, cs, cs), 2)

    for h in range(N_REC_HEADS):
        hs = slice(h * d, (h + 1) * d)
        b = lf_ref[0, :, hs]
        shift = 1
        while shift < cs:
            b = b + jnp.where(row2 >= shift, pltpu.roll(b, shift, 0), 0.0)
            shift *= 2
        b3 = b.reshape(nc, cs, d)
        q3 = q_ref[0, :, hs].astype(F32).reshape(nc, cs, d)
        k3 = k_ref[0, :, hs].astype(F32).reshape(nc, cs, d)
        v3b = v_ref[0, :, hs].reshape(nc, cs, d)
        v3 = v3b.astype(F32)
        b_last = jnp.broadcast_to(b3[:, cs - 1:cs, :], b3.shape)

        scores = jnp.zeros((nc, cs, cs), F32)
        level = cs // 2
        while level >= 1:
            ref = _reference_rows(b3, level)
            is_q = (row3 % (2 * level)) >= level
            e = jnp.exp(jnp.where(is_q, b3 - ref, ref - b3))
            xl = (jnp.where(is_q, q3, k3) * e).astype(BF16)
            g = jnp.einsum("ctd,csd->cts", xl, xl, preferred_element_type=F32)
            t_q = (t_i % (2 * level)) >= level
            s_k = (s_i % (2 * level)) < level
            same = (t_i // (2 * level)) == (s_i // (2 * level))
            scores = scores + jnp.where(t_q & s_k & same, g, 0.0)
            level //= 2
        diag = jnp.sum(q3 * k3, axis=-1, keepdims=True)
        o_intra = jnp.einsum("cts,cse->cte", scores.astype(BF16), v3b,
                             preferred_element_type=F32) + diag * v3

        q_dec = (q3 * jnp.exp(b3)).astype(BF16)
        k_dec = (k3 * jnp.exp(b_last - b3)).astype(BF16)

        st = st_ref[h]
        outs = []
        for c in range(nc):
            o_c = o_intra[c] + lax.dot_general(q_dec[c], st.astype(BF16), (((1,), (1,)), ((), ())),
                                               preferred_element_type=F32)
            outs.append(o_c)
            upd = lax.dot_general(v3b[c], k_dec[c], (((0,), (0,)), ((), ())),
                                  preferred_element_type=F32)
            st = st * jnp.exp(b3[c, cs - 1:cs, :]) + upd
        st_ref[h] = st
        o = jnp.concatenate(outs, axis=0)
        ms = jnp.mean(o * o, axis=-1, keepdims=True)
        y = o * lax.rsqrt(ms + RMS_EPS) * ng_ref[:, hs] * g_ref[0, :, hs].astype(F32)
        o_ref[0, :, hs] = y.astype(o_ref.dtype)


def _hgrn(qr, kr, ir, lfr, gr, norm_g, *, layer):
    B, S, W = qr.shape
    ts = REC_TOKENS
    tok = pl.BlockSpec((1, ts, W), lambda b, s: (b, s, 0))
    return pl.pallas_call(
        functools.partial(_hgrn_kernel, ts=ts),
        grid=(B, S // ts),
        in_specs=[tok, tok, tok, tok, tok, _resident((1, W))],
        out_specs=tok,
        out_shape=jax.ShapeDtypeStruct((B, S, W), BF16),
        scratch_shapes=[pltpu.VMEM((N_REC_HEADS, REC_HEAD_DIM, REC_HEAD_DIM), F32)],
        compiler_params=pltpu.CompilerParams(
            dimension_semantics=("arbitrary", "arbitrary"), vmem_limit_bytes=VMEM_LIMIT_BYTES),
        name=f"hgrn_l{layer}",
    )(qr, kr, ir, lfr, gr, norm_g)


def _out_kernel(oa_ref, or_ref, x_ref, wa_ref, wr_ref, g_ref, b_ref, y_ref, *, alpha):
    mix = jnp.dot(oa_ref[...], wa_ref[...], preferred_element_type=F32)
    mix = mix + jnp.dot(or_ref[...], wr_ref[...], preferred_element_type=F32)
    y_ref[...] = _layer_norm(alpha * x_ref[...] + mix, g_ref[...], b_ref[...])


def _out_project(oa, orr, x, wa, wr, g, b, *, alpha, layer):
    N, D = x.shape
    tm = OUT_TOKENS
    return pl.pallas_call(
        functools.partial(_out_kernel, alpha=alpha),
        grid=(N // tm,),
        in_specs=[pl.BlockSpec((tm, ATTN_WIDTH), lambda i: (i, 0)),
                  pl.BlockSpec((tm, REC_WIDTH), lambda i: (i, 0)),
                  pl.BlockSpec((tm, D), lambda i: (i, 0)),
                  _resident(wa.shape), _resident(wr.shape), _resident((1, D)), _resident((1, D))],
        out_specs=pl.BlockSpec((tm, D), lambda i: (i, 0)),
        out_shape=jax.ShapeDtypeStruct((N, D), F32),
        compiler_params=pltpu.CompilerParams(
            dimension_semantics=("arbitrary",), vmem_limit_bytes=VMEM_LIMIT_BYTES),
        name=f"outproj_l{layer}",
    )(oa, orr, x, wa, wr, g, b)


def _ffn_kernel(x_ref, halo_ref, wa_ref, wu_ref, cv_ref, wd_ref, g_ref, b_ref, y_ref, acc_ref,
                *, alpha, tm, tiles_per_seq, n_blocks):
    i = pl.program_id(0)
    x = x_ref[...]
    halo = jnp.where(i % tiles_per_seq == 0, 0.0, halo_ref[...])
    xe = jnp.concatenate([halo, x], axis=0).astype(BF16)
    acc_ref[...] = jnp.zeros_like(acc_ref)

    def conv(h, cw):
        y = cw[CONV_WIDTH:CONV_WIDTH + 1, :]
        for j in range(CONV_WIDTH):
            off = HALO - (CONV_WIDTH - 1) + j
            y = y + cw[j:j + 1, :] * h[off:off + tm, :]
        return y

    def block(j, carry):
        ha = jnp.dot(xe, wa_ref[j], preferred_element_type=F32)
        hu = jnp.dot(xe, wu_ref[j], preferred_element_type=F32)
        a = conv(ha, cv_ref[0, j])
        u = conv(hu, cv_ref[1, j])
        gate = 0.5 * a * (1.0 + lax.erf(a * (2.0 ** -0.5)))
        acc_ref[...] += jnp.dot((gate * u).astype(BF16), wd_ref[j], preferred_element_type=F32)
        return carry

    lax.fori_loop(0, n_blocks, block, 0)
    y_ref[...] = _layer_norm(alpha * x + acc_ref[...], g_ref[...], b_ref[...])


def _ffn(x, wa, wu, cv, wd, g, b, *, alpha, seq, layer):
    N, D = x.shape
    tm = FFN_TOKENS
    n_blocks = wa.shape[0]
    halo_blocks = tm // HALO
    return pl.pallas_call(
        functools.partial(_ffn_kernel, alpha=alpha, tm=tm, tiles_per_seq=seq // tm, n_blocks=n_blocks),
        grid=(N // tm,),
        in_specs=[pl.BlockSpec((tm, D), lambda i: (i, 0)),
                  pl.BlockSpec((HALO, D), lambda i: (jnp.maximum(i * halo_blocks - 1, 0), 0)),
                  _resident(wa.shape), _resident(wu.shape), _resident(cv.shape), _resident(wd.shape),
                  _resident((1, D)), _resident((1, D))],
        out_specs=pl.BlockSpec((tm, D), lambda i: (i, 0)),
        out_shape=jax.ShapeDtypeStruct((N, D), F32),
        scratch_shapes=[pltpu.VMEM((tm, D), F32)],
        compiler_params=pltpu.CompilerParams(
            dimension_semantics=("arbitrary",), vmem_limit_bytes=VMEM_LIMIT_BYTES),
        name=f"ffn_l{layer}",
    )(x, x, wa, wu, cv, wd, g, b)


def _pack_conv(conv_w, conv_b, d_ff):
    rows = jnp.concatenate([conv_w, conv_b[None, :],
                            jnp.zeros((SUBLANES - CONV_WIDTH - 1, conv_w.shape[1]), conv_w.dtype)], axis=0)
    rows = rows.reshape(SUBLANES, 2, d_ff // FFN_BLOCK, FFN_BLOCK)
    return jnp.transpose(rows, (1, 2, 0, 3))


def kernel(x, ln_emb_g, ln_emb_b, w_in, fox_f_bias, fox_norm_g, hgrn_lower_bounds, hgrn_norm_g,
           w_o, ln_mix_g, ln_mix_b, w_up, conv_w, conv_b, w_down, ln_ffn_g, ln_ffn_b):
    B, S, D = x.shape
    depth = w_in.shape[0]
    d_ff = w_down.shape[1]
    alpha = (2.0 * depth) ** 0.25
    n_blocks = d_ff // FFN_BLOCK
    aw, rw = ATTN_WIDTH, REC_WIDTH
    row = lambda a: a.reshape(1, -1)

    xs = x
    for l in range(depth):
        w = w_in[l]
        wqkv = w[:, :3 * aw].astype(BF16)
        wfa = w[:, 3 * aw:3 * aw + N_ATTN_HEADS]
        wfat = jnp.concatenate([wfa.T, jnp.zeros((16 - N_ATTN_HEADS, D), w.dtype)], axis=0).astype(BF16)
        wr = w[:, 3 * aw + N_ATTN_HEADS:].astype(BF16)
        fbias = fox_f_bias[l].reshape(N_ATTN_HEADS, 1)
        if l == 0:
            outs = _project(xs, row(ln_emb_g), row(ln_emb_b), wqkv, wfat, fbias, wr,
                            hgrn_lower_bounds, layer=l)
            xs, outs = outs[0], outs[1:]
        else:
            outs = _project(xs, None, None, wqkv, wfat, fbias, wr, hgrn_lower_bounds, layer=l)
        q, k, v, ct, qr, kr, ir, gr, lfr = outs

        o_a = _fox_attention(q, k, v, ct, row(fox_norm_g[l]), layer=l)
        o_r = _hgrn(qr, kr, ir, lfr, gr, row(hgrn_norm_g[l]), layer=l)

        wo = w_o[l].astype(BF16)
        x2 = xs.reshape(B * S, D)
        x2 = _out_project(o_a.reshape(B * S, aw), o_r.reshape(B * S, rw), x2, wo[:aw], wo[aw:],
                          row(ln_mix_g[l]), row(ln_mix_b[l]), alpha=alpha, layer=l)

        wup = w_up[l].astype(BF16)
        wa = jnp.transpose(wup[:, :d_ff].reshape(D, n_blocks, FFN_BLOCK), (1, 0, 2))
        wu = jnp.transpose(wup[:, d_ff:].reshape(D, n_blocks, FFN_BLOCK), (1, 0, 2))
        wd = w_down[l].astype(BF16).reshape(n_blocks, FFN_BLOCK, D)
        cv = _pack_conv(conv_w[l], conv_b[l], d_ff)
        x2 = _ffn(x2, wa, wu, cv, wd, row(ln_ffn_g[l]), row(ln_ffn_b[l]), alpha=alpha, seq=S, layer=l)
        xs = x2.reshape(B, S, D)
    return xs
```

```python
import functools
import math

import jax
import jax.numpy as jnp
from jax import lax
from jax.experimental import pallas as pl
from jax.experimental.pallas import tpu as pltpu

F32 = jnp.float32
BF16 = jnp.bfloat16

N_ATTN_HEADS = 8
ATTN_HEAD_DIM = 64
ATTN_WIDTH = N_ATTN_HEADS * ATTN_HEAD_DIM
N_REC_HEADS = 4
REC_HEAD_DIM = 128
REC_WIDTH = N_REC_HEADS * REC_HEAD_DIM
CONV_WIDTH = 3
LN_EPS = 1e-5
RMS_EPS = 1e-6
LOG2E = math.log2(math.e)
Q_SCALE = LOG2E * ATTN_HEAD_DIM ** -0.5

LANES = 128
SUBLANES = 8
VMEM_LIMIT_BYTES = 56 * 1024 * 1024

PROJ_TOKENS = 512
ATTN_TILE = 512
REC_TOKENS = 512
REC_CHUNK = 64
OUT_TOKENS = 512
FFN_TOKENS = 512
FFN_BLOCK = 256
HALO = SUBLANES


def _resident(shape):
    nd = len(shape)
    return pl.BlockSpec(shape, lambda *_: (0,) * nd, pipeline_mode=pl.Buffered(1))


def _layer_norm(x, g, b):
    mu = jnp.mean(x, axis=-1, keepdims=True)
    xc = x - mu
    var = jnp.mean(xc * xc, axis=-1, keepdims=True)
    return xc * lax.rsqrt(var + LN_EPS) * g + b


def _log_sigmoid(z):
    return jnp.minimum(z, 0.0) - jnp.log(1.0 + jnp.exp(-jnp.abs(z)))


def _sigmoid(z):
    return 1.0 / (1.0 + jnp.exp(-z))


def _silu(z):
    return z * _sigmoid(z)


def _proj_kernel(*refs, apply_ln, layer, tm):
    if apply_ln:
        x_ref, lng_ref, lnb_ref = refs[:3]
        refs = refs[3:]
    else:
        x_ref = refs[0]
        refs = refs[1:]
    (wqkv_ref, fbias_ref, wr_ref, lbraw_ref) = refs[:4]
    refs = refs[4:]
    if apply_ln:
        xln_ref = refs[0]
        refs = refs[1:]
    (q_ref, k_ref, v_ref, qr_ref, kr_ref, ir_ref, gr_ref, lfr_ref, carry_ref) = refs

    s_idx = pl.program_id(1)
    x = x_ref[0]
    if apply_ln:
        x = _layer_norm(x, lng_ref[...], lnb_ref[...])
        xln_ref[0] = x
    xb = x.astype(BF16)

    qkv = jnp.dot(xb, wqkv_ref[...], preferred_element_type=F32)
    logf = _log_sigmoid(qkv[:, 3 * ATTN_WIDTH:] + fbias_ref[...])
    row = lax.broadcasted_iota(jnp.int32, logf.shape, 0)
    c = logf
    shift = 1
    while shift < tm:
        c = c + jnp.where(row >= shift, pltpu.roll(c, shift, 0), 0.0)
        shift *= 2

    @pl.when(s_idx == 0)
    def _():
        carry_ref[...] = jnp.zeros_like(carry_ref)

    c = c + carry_ref[0:1, :]
    carry_ref[...] = jnp.broadcast_to(c[tm - 1:tm, :], carry_ref.shape)
    c2 = c * LOG2E

    lane = lax.broadcasted_iota(jnp.int32, (tm, LANES), 1)
    is_data = lane < ATTN_HEAD_DIM
    v_extra = jnp.where(lane == ATTN_HEAD_DIM, 1.0, 0.0)
    for h in range(N_ATTN_HEADS):
        cb = jnp.broadcast_to(c2[:, h:h + 1], (tm, LANES))
        hi = cb.astype(BF16).astype(F32)
        rem = cb - hi
        mid = rem.astype(BF16).astype(F32)
        lo = rem - mid
        q_extra = jnp.where(lane == 64, hi, jnp.where(lane == 65, mid, jnp.where(
            lane == 66, lo, jnp.where(lane < 70, 1.0, 0.0))))
        k_extra = jnp.where(lane < 67, 1.0, jnp.where(lane == 67, -hi, jnp.where(
            lane == 68, -mid, jnp.where(lane == 69, -lo, 0.0))))
        slab = slice(h * LANES, (h + 1) * LANES)

        def head_cols(base):
            blk = qkv[:, base + LANES * (h // 2):base + LANES * (h // 2 + 1)]
            return blk if h % 2 == 0 else pltpu.roll(blk, ATTN_HEAD_DIM, 1)

        q_ref[0, :, slab] = jnp.where(is_data, head_cols(0) * Q_SCALE, q_extra).astype(BF16)
        k_ref[0, :, slab] = jnp.where(is_data, head_cols(ATTN_WIDTH), k_extra).astype(BF16)
        v_ref[0, :, slab] = jnp.where(is_data, head_cols(2 * ATTN_WIDTH), v_extra).astype(BF16)

    r = jnp.dot(xb, wr_ref[...], preferred_element_type=F32)
    qr_ref[0] = _silu(r[:, :REC_WIDTH]).astype(BF16)
    f_r = r[:, REC_WIDTH:2 * REC_WIDTH]
    ls = _log_sigmoid(f_r)
    if layer == 0:
        lfr_ref[0] = ls
        kr_ref[0] = _sigmoid(-f_r).astype(BF16)
    else:
        raw = lbraw_ref[...]
        e = jnp.exp(raw - jnp.max(raw, axis=0, keepdims=True))
        sm = e / jnp.sum(e, axis=0, keepdims=True)
        lb = jnp.sum(sm[1:layer + 1], axis=0, keepdims=True)
        a = jnp.log(lb)
        b = jnp.log(1.0 - lb) + ls
        hi = jnp.maximum(a, b)
        lfr_ref[0] = hi + jnp.log(1.0 + jnp.exp(-jnp.abs(a - b)))
        kr_ref[0] = ((1.0 - lb) * _sigmoid(-f_r)).astype(BF16)
    ir_ref[0] = r[:, 2 * REC_WIDTH:3 * REC_WIDTH].astype(BF16)
    gr_ref[0] = _silu(r[:, 3 * REC_WIDTH:]).astype(BF16)


def _project(x, ln_g, ln_b, wqkv, fbias, wr, lbraw, *, layer):
    B, S, D = x.shape
    tm = PROJ_TOKENS
    apply_ln = ln_g is not None
    depth = lbraw.shape[0]
    slab_width = N_ATTN_HEADS * LANES
    tok = lambda w: pl.BlockSpec((1, tm, w), lambda b, s: (b, s, 0))
    in_specs = [tok(D)]
    args = [x]
    if apply_ln:
        in_specs += [_resident((1, D)), _resident((1, D))]
        args += [ln_g, ln_b]
    in_specs += [_resident(wqkv.shape), _resident(fbias.shape), _resident(wr.shape),
                 _resident((depth, REC_WIDTH))]
    args += [wqkv, fbias, wr, lbraw]
    out_shape, out_specs = [], []
    if apply_ln:
        out_shape.append(jax.ShapeDtypeStruct((B, S, D), F32))
        out_specs.append(tok(D))
    for _ in range(3):
        out_shape.append(jax.ShapeDtypeStruct((B, S, slab_width), BF16))
        out_specs.append(tok(slab_width))
    for _ in range(4):
        out_shape.append(jax.ShapeDtypeStruct((B, S, REC_WIDTH), BF16))
        out_specs.append(tok(REC_WIDTH))
    out_shape.append(jax.ShapeDtypeStruct((B, S, REC_WIDTH), F32))
    out_specs.append(tok(REC_WIDTH))
    outs = pl.pallas_call(
        functools.partial(_proj_kernel, apply_ln=apply_ln, layer=layer, tm=tm),
        grid=(B, S // tm),
        in_specs=in_specs,
        out_specs=out_specs,
        out_shape=out_shape,
        scratch_shapes=[pltpu.VMEM((SUBLANES, LANES), F32)],
        compiler_params=pltpu.CompilerParams(
            dimension_semantics=("arbitrary", "arbitrary"), vmem_limit_bytes=VMEM_LIMIT_BYTES),
        name=f"proj_l{layer}",
    )(*args)
    return outs


def _fox_kernel(q_ref, k_ref, v_ref, g_ref, o_ref, m_ref, acc_ref, *, t):
    qi = pl.program_id(2)
    m_ref[...] = jnp.full_like(m_ref, -jnp.inf)
    acc_ref[...] = jnp.zeros_like(acc_ref)
    q = [q_ref[0, :, h * LANES:(h + 1) * LANES] for h in range(2)]

    def tile(kj, masked):
        start = pl.multiple_of(kj * t, t)
        for h in range(2):
            k = k_ref[0, pl.ds(start, t), h * LANES:(h + 1) * LANES]
            v = v_ref[0, pl.ds(start, t), h * LANES:(h + 1) * LANES]
            s = lax.dot_general(q[h], k, (((1,), (1,)), ((), ())), preferred_element_type=F32)
            if masked:
                r_i = lax.broadcasted_iota(jnp.int32, s.shape, 0)
                c_i = lax.broadcasted_iota(jnp.int32, s.shape, 1)
                s = jnp.where(c_i <= r_i, s, -jnp.inf)
            m_prev = m_ref[h]
            m_next = jnp.maximum(m_prev, jnp.max(s, axis=-1, keepdims=True))
            p = jnp.exp2(s - jnp.tile(m_next, (1, t // LANES)))
            alpha = jnp.exp2(m_prev - m_next)
            acc_ref[h] = alpha * acc_ref[h] + jnp.dot(p.astype(BF16), v, preferred_element_type=F32)
            m_ref[h] = m_next

    def body(kj, carry):
        tile(kj, False)
        return carry

    lax.fori_loop(0, qi, body, 0)
    tile(qi, True)

    def normalised(h):
        acc = acc_ref[h]
        return acc / jnp.broadcast_to(acc[:, ATTN_HEAD_DIM:ATTN_HEAD_DIM + 1], acc.shape)

    lane_o = lax.broadcasted_iota(jnp.int32, (t, LANES), 1)
    first_o = lane_o < ATTN_HEAD_DIM
    o = jnp.where(first_o, normalised(0), pltpu.roll(normalised(1), ATTN_HEAD_DIM, 1))
    sq = o * o
    ms0 = jnp.sum(jnp.where(first_o, sq, 0.0), axis=-1, keepdims=True) / ATTN_HEAD_DIM
    ms1 = jnp.sum(jnp.where(first_o, 0.0, sq), axis=-1, keepdims=True) / ATTN_HEAD_DIM
    scale = jnp.where(first_o, lax.rsqrt(ms0 + RMS_EPS), lax.rsqrt(ms1 + RMS_EPS))
    o_ref[0] = (o * scale * g_ref[...]).astype(o_ref.dtype)


def _fox_attention(q, k, v, norm_g, *, layer):
    B, S, _ = q.shape
    t = ATTN_TILE
    pairs = N_ATTN_HEADS // 2
    return pl.pallas_call(
        functools.partial(_fox_kernel, t=t),
        grid=(B, pairs, S // t),
        in_specs=[
            pl.BlockSpec((1, t, 2 * LANES), lambda b, p, i: (b, i, p)),
            pl.BlockSpec((1, S, 2 * LANES), lambda b, p, i: (b, 0, p)),
            pl.BlockSpec((1, S, 2 * LANES), lambda b, p, i: (b, 0, p)),
            pl.BlockSpec((1, LANES), lambda b, p, i: (0, p)),
        ],
        out_specs=pl.BlockSpec((1, t, LANES), lambda b, p, i: (b, i, p)),
        out_shape=jax.ShapeDtypeStruct((B, S, ATTN_WIDTH), BF16),
        scratch_shapes=[pltpu.VMEM((2, t, LANES), F32), pltpu.VMEM((2, t, LANES), F32)],
        compiler_params=pltpu.CompilerParams(
            dimension_semantics=("arbitrary", "arbitrary", "arbitrary"),
            vmem_limit_bytes=VMEM_LIMIT_BYTES),
        name=f"fox_l{layer}",
    )(q, k, v, norm_g)


def _reference_rows(b3, level):
    nc, cs, d = b3.shape
    if level >= SUBLANES:
        parts = []
        for g0 in range(0, cs, 2 * level):
            row = b3[:, g0 + level:g0 + level + 1, :]
            parts.append(jnp.broadcast_to(row, (nc, 2 * level, d)))
        return jnp.concatenate(parts, axis=1)
    b4 = b3.reshape(nc * cs // SUBLANES, SUBLANES, d)
    sub = lax.broadcasted_iota(jnp.int32, b4.shape, 1)
    ref = jnp.broadcast_to(b4[:, SUBLANES - level:SUBLANES - level + 1, :], b4.shape)
    for g0 in range(SUBLANES - 4 * level, -1, -2 * level):
        row = jnp.broadcast_to(b4[:, g0 + level:g0 + level + 1, :], b4.shape)
        ref = jnp.where(sub < g0 + 2 * level, row, ref)
    return ref.reshape(nc, cs, d)


def _hgrn_kernel(q_ref, k_ref, v_ref, lf_ref, g_ref, ng_ref, o_ref, st_ref, *, ts):
    cs = REC_CHUNK
    nc = ts // cs
    d = REC_HEAD_DIM

    @pl.when(pl.program_id(1) == 0)
    def _():
        st_ref[...] = jnp.zeros_like(st_ref)

    row2 = lax.broadcasted_iota(jnp.int32, (ts, d), 0) % cs
    row3 = lax.broadcasted_iota(jnp.int32, (nc, cs, d), 1)
    t_i = lax.broadcasted_iota(jnp.int32, (nc, cs, cs), 1)
    s_i = lax.broadcasted_iota(jnp.int32, (nc, cs, cs), 2)

    for h in range(N_REC_HEADS):
        hs = slice(h * d, (h + 1) * d)
        b = lf_ref[0, :, hs]
        shift = 1
        while shift < cs:
            b = b + jnp.where(row2 >= shift, pltpu.roll(b, shift, 0), 0.0)
            shift *= 2
        b3 = b.reshape(nc, cs, d)
        q3 = q_ref[0, :, hs].astype(F32).reshape(nc, cs, d)
        k3 = k_ref[0, :, hs].astype(F32).reshape(nc, cs, d)
        v3b = v_ref[0, :, hs].reshape(nc, cs, d)
        v3 = v3b.astype(F32)
        b_last = jnp.broadcast_to(b3[:, cs - 1:cs, :], b3.shape)

        scores = jnp.zeros((nc, cs, cs), F32)
        level = cs // 2
        while level >= 1:
            ref = _reference_rows(b3, level)
            is_q = (row3 % (2 * level)) >= level
            e = jnp.exp(jnp.where(is_q, b3 - ref, ref - b3))
            xl = (jnp.where(is_q, q3, k3) * e).astype(BF16)
            g = jnp.einsum("ctd,csd->cts", xl, xl, preferred_element_type=F32)
            t_q = (t_i % (2 * level)) >= level
            s_k = (s_i % (2 * level)) < level
            same = (t_i // (2 * level)) == (s_i // (2 * level))
            scores = scores + jnp.where(t_q & s_k & same, g, 0.0)
            level //= 2
        diag = jnp.sum(q3 * k3, axis=-1, keepdims=True)
        o_intra = jnp.einsum("cts,cse->cte", scores.astype(BF16), v3b,
                             preferred_element_type=F32) + diag * v3

        q_dec = (q3 * jnp.exp(b3)).astype(BF16)
        k_dec = (k3 * jnp.exp(b_last - b3)).astype(BF16)

        st = st_ref[h]
        outs = []
        for c in range(nc):
            o_c = o_intra[c] + lax.dot_general(q_dec[c], st.astype(BF16), (((1,), (1,)), ((), ())),
                                               preferred_element_type=F32)
            outs.append(o_c)
            upd = lax.dot_general(v3b[c], k_dec[c], (((0,), (0,)), ((), ())),
                                  preferred_element_type=F32)
            st = st * jnp.exp(b3[c, cs - 1:cs, :]) + upd
        st_ref[h] = st
        o = jnp.concatenate(outs, axis=0)
        ms = jnp.mean(o * o, axis=-1, keepdims=True)
        y = o * lax.rsqrt(ms + RMS_EPS) * ng_ref[:, hs] * g_ref[0, :, hs].astype(F32)
        o_ref[0, :, hs] = y.astype(o_ref.dtype)


def _hgrn(qr, kr, ir, lfr, gr, norm_g, *, layer):
    B, S, W = qr.shape
    ts = REC_TOKENS
    tok = pl.BlockSpec((1, ts, W), lambda b, s: (b, s, 0))
    return pl.pallas_call(
        functools.partial(_hgrn_kernel, ts=ts),
        grid=(B, S // ts),
        in_specs=[tok, tok, tok, tok, tok, _resident((1, W))],
        out_specs=tok,
        out_shape=jax.ShapeDtypeStruct((B, S, W), BF16),
        scratch_shapes=[pltpu.VMEM((N_REC_HEADS, REC_HEAD_DIM, REC_HEAD_DIM), F32)],
        compiler_params=pltpu.CompilerParams(
            dimension_semantics=("arbitrary", "arbitrary"), vmem_limit_bytes=VMEM_LIMIT_BYTES),
        name=f"hgrn_l{layer}",
    )(qr, kr, ir, lfr, gr, norm_g)


def _out_kernel(oa_ref, or_ref, x_ref, wa_ref, wr_ref, g_ref, b_ref, y_ref, *, alpha):
    mix = jnp.dot(oa_ref[...], wa_ref[...], preferred_element_type=F32)
    mix = mix + jnp.dot(or_ref[...], wr_ref[...], preferred_element_type=F32)
    y_ref[...] = _layer_norm(alpha * x_ref[...] + mix, g_ref[...], b_ref[...])


def _out_project(oa, orr, x, wa, wr, g, b, *, alpha, layer):
    N, D = x.shape
    tm = OUT_TOKENS
    return pl.pallas_call(
        functools.partial(_out_kernel, alpha=alpha),
        grid=(N // tm,),
        in_specs=[pl.BlockSpec((tm, ATTN_WIDTH), lambda i: (i, 0)),
                  pl.BlockSpec((tm, REC_WIDTH), lambda i: (i, 0)),
                  pl.BlockSpec((tm, D), lambda i: (i, 0)),
                  _resident(wa.shape), _resident(wr.shape), _resident((1, D)), _resident((1, D))],
        out_specs=pl.BlockSpec((tm, D), lambda i: (i, 0)),
        out_shape=jax.ShapeDtypeStruct((N, D), F32),
        compiler_params=pltpu.CompilerParams(
            dimension_semantics=("arbitrary",), vmem_limit_bytes=VMEM_LIMIT_BYTES),
        name=f"outproj_l{layer}",
    )(oa, orr, x, wa, wr, g, b)


def _ffn_kernel(x_ref, halo_ref, wa_ref, wu_ref, cv_ref, wd_ref, g_ref, b_ref, y_ref, acc_ref,
                *, alpha, tm, tiles_per_seq, n_blocks):
    i = pl.program_id(0)
    x = x_ref[...]
    halo = jnp.where(i % tiles_per_seq == 0, 0.0, halo_ref[...])
    xe = jnp.concatenate([halo, x], axis=0).astype(BF16)
    acc_ref[...] = jnp.zeros_like(acc_ref)

    def conv(h, cw):
        y = cw[CONV_WIDTH:CONV_WIDTH + 1, :]
        for j in range(CONV_WIDTH):
            off = HALO - (CONV_WIDTH - 1) + j
            y = y + cw[j:j + 1, :] * h[off:off + tm, :]
        return y

    def block(j, carry):
        ha = jnp.dot(xe, wa_ref[j], preferred_element_type=F32)
        hu = jnp.dot(xe, wu_ref[j], preferred_element_type=F32)
        a = conv(ha, cv_ref[0, j])
        u = conv(hu, cv_ref[1, j])
        gate = 0.5 * a * (1.0 + lax.erf(a * (2.0 ** -0.5)))
        acc_ref[...] += jnp.dot((gate * u).astype(BF16), wd_ref[j], preferred_element_type=F32)
        return carry

    lax.fori_loop(0, n_blocks, block, 0)
    y_ref[...] = _layer_norm(alpha * x + acc_ref[...], g_ref[...], b_ref[...])


def _ffn(x, wa, wu, cv, wd, g, b, *, alpha, seq, layer):
    N, D = x.shape
    tm = FFN_TOKENS
    n_blocks = wa.shape[0]
    halo_blocks = tm // HALO
    return pl.pallas_call(
        functools.partial(_ffn_kernel, alpha=alpha, tm=tm, tiles_per_seq=seq // tm, n_blocks=n_blocks),
        grid=(N // tm,),
        in_specs=[pl.BlockSpec((tm, D), lambda i: (i, 0)),
                  pl.BlockSpec((HALO, D), lambda i: (jnp.maximum(i * halo_blocks - 1, 0), 0)),
                  _resident(wa.shape), _resident(wu.shape), _resident(cv.shape), _resident(wd.shape),
                  _resident((1, D)), _resident((1, D))],
        out_specs=pl.BlockSpec((tm, D), lambda i: (i, 0)),
        out_shape=jax.ShapeDtypeStruct((N, D), F32),
        scratch_shapes=[pltpu.VMEM((tm, D), F32)],
        compiler_params=pltpu.CompilerParams(
            dimension_semantics=("arbitrary",), vmem_limit_bytes=VMEM_LIMIT_BYTES),
        name=f"ffn_l{layer}",
    )(x, x, wa, wu, cv, wd, g, b)


def _pack_conv(conv_w, conv_b, d_ff):
    rows = jnp.concatenate([conv_w, conv_b[None, :],
                            jnp.zeros((SUBLANES - CONV_WIDTH - 1, conv_w.shape[1]), conv_w.dtype)], axis=0)
    rows = rows.reshape(SUBLANES, 2, d_ff // FFN_BLOCK, FFN_BLOCK)
    return jnp.transpose(rows, (1, 2, 0, 3))


def kernel(x, ln_emb_g, ln_emb_b, w_in, fox_f_bias, fox_norm_g, hgrn_lower_bounds, hgrn_norm_g,
           w_o, ln_mix_g, ln_mix_b, w_up, conv_w, conv_b, w_down, ln_ffn_g, ln_ffn_b):
    B, S, D = x.shape
    depth = w_in.shape[0]
    d_ff = w_down.shape[1]
    alpha = (2.0 * depth) ** 0.25
    n_blocks = d_ff // FFN_BLOCK
    aw, rw = ATTN_WIDTH, REC_WIDTH
    row = lambda a: a.reshape(1, -1)

    xs = x
    for l in range(depth):
        w = w_in[l]
        gate_pad = jnp.zeros((D, LANES - N_ATTN_HEADS), w.dtype)
        wqkv = jnp.concatenate([w[:, :3 * aw + N_ATTN_HEADS], gate_pad], axis=1).astype(BF16)
        wr = w[:, 3 * aw + N_ATTN_HEADS:].astype(BF16)
        fbias = jnp.pad(fox_f_bias[l], (0, LANES - N_ATTN_HEADS)).reshape(1, LANES)
        if l == 0:
            outs = _project(xs, row(ln_emb_g), row(ln_emb_b), wqkv, fbias, wr,
                            hgrn_lower_bounds, layer=l)
            xs, outs = outs[0], outs[1:]
        else:
            outs = _project(xs, None, None, wqkv, fbias, wr, hgrn_lower_bounds, layer=l)
        q, k, v, qr, kr, ir, gr, lfr = outs

        o_a = _fox_attention(q, k, v, row(fox_norm_g[l]), layer=l)
        o_r = _hgrn(qr, kr, ir, lfr, gr, row(hgrn_norm_g[l]), layer=l)

        wo = w_o[l].astype(BF16)
        x2 = xs.reshape(B * S, D)
        x2 = _out_project(o_a.reshape(B * S, aw), o_r.reshape(B * S, rw), x2, wo[:aw], wo[aw:],
                          row(ln_mix_g[l]), row(ln_mix_b[l]), alpha=alpha, layer=l)

        wup = w_up[l].astype(BF16)
        wa = jnp.transpose(wup[:, :d_ff].reshape(D, n_blocks, FFN_BLOCK), (1, 0, 2))
        wu = jnp.transpose(wup[:, d_ff:].reshape(D, n_blocks, FFN_BLOCK), (1, 0, 2))
        wd = w_down[l].astype(BF16).reshape(n_blocks, FFN_BLOCK, D)
        cv = _pack_conv(conv_w[l], conv_b[l], d_ff)
        x2 = _ffn(x2, wa, wu, cv, wd, row(ln_ffn_g[l]), row(ln_ffn_b[l]), alpha=alpha, seq=S, layer=l)
        xs = x2.reshape(B, S, D)
    return xs
```

```python
import functools
import math

import jax
import jax.numpy as jnp
from jax import lax
from jax.experimental import pallas as pl
from jax.experimental.pallas import tpu as pltpu

F32 = jnp.float32
BF16 = jnp.bfloat16

N_ATTN_HEADS = 8
ATTN_HEAD_DIM = 64
ATTN_WIDTH = N_ATTN_HEADS * ATTN_HEAD_DIM
N_REC_HEADS = 4
REC_HEAD_DIM = 128
REC_WIDTH = N_REC_HEADS * REC_HEAD_DIM
CONV_WIDTH = 3
LN_EPS = 1e-5
RMS_EPS = 1e-6
LOG2E = math.log2(math.e)
Q_SCALE = LOG2E * ATTN_HEAD_DIM ** -0.5

LANES = 128
SUBLANES = 8
VMEM_LIMIT_BYTES = 56 * 1024 * 1024

PROJ_TOKENS = 512
ATTN_TILE = 512
REC_TOKENS = 512
REC_CHUNK = 64
OUT_TOKENS = 512
FFN_TOKENS = 512
FFN_BLOCK = 256
HALO = SUBLANES


def _resident(shape):
    nd = len(shape)
    return pl.BlockSpec(shape, lambda *_: (0,) * nd, pipeline_mode=pl.Buffered(1))


def _layer_norm(x, g, b):
    mu = jnp.mean(x, axis=-1, keepdims=True)
    xc = x - mu
    var = jnp.mean(xc * xc, axis=-1, keepdims=True)
    return xc * lax.rsqrt(var + LN_EPS) * g + b


def _log_sigmoid(z):
    return jnp.minimum(z, 0.0) - jnp.log(1.0 + jnp.exp(-jnp.abs(z)))


def _sigmoid(z):
    return 1.0 / (1.0 + jnp.exp(-z))


def _silu(z):
    return z * _sigmoid(z)


def _proj_kernel(*refs, apply_ln, layer, tm):
    if apply_ln:
        x_ref, lng_ref, lnb_ref = refs[:3]
        refs = refs[3:]
    else:
        x_ref = refs[0]
        refs = refs[1:]
    (wqkv_ref, fbias_ref, wr_ref, lbraw_ref) = refs[:4]
    refs = refs[4:]
    if apply_ln:
        xln_ref = refs[0]
        refs = refs[1:]
    (q_ref, k_ref, v_ref, qr_ref, kr_ref, ir_ref, gr_ref, lfr_ref, carry_ref) = refs

    s_idx = pl.program_id(1)
    x = x_ref[0]
    if apply_ln:
        x = _layer_norm(x, lng_ref[...], lnb_ref[...])
        xln_ref[0] = x
    xb = x.astype(BF16)

    qkv = jnp.dot(xb, wqkv_ref[...], preferred_element_type=F32)
    logf = _log_sigmoid(qkv[:, 3 * ATTN_WIDTH:] + fbias_ref[...])
    row = lax.broadcasted_iota(jnp.int32, logf.shape, 0)
    c = logf
    shift = 1
    while shift < tm:
        c = c + jnp.where(row >= shift, pltpu.roll(c, shift, 0), 0.0)
        shift *= 2

    @pl.when(s_idx == 0)
    def _():
        carry_ref[...] = jnp.zeros_like(carry_ref)

    c = c + carry_ref[0:1, :]
    carry_ref[...] = jnp.broadcast_to(c[tm - 1:tm, :], carry_ref.shape)
    c2 = c * LOG2E

    lane = lax.broadcasted_iota(jnp.int32, (tm, LANES), 1)
    is_data = lane < ATTN_HEAD_DIM
    v_extra = jnp.where(lane == ATTN_HEAD_DIM, 1.0, 0.0)
    for h in range(N_ATTN_HEADS):
        cb = jnp.broadcast_to(c2[:, h:h + 1], (tm, LANES))
        hi = cb.astype(BF16).astype(F32)
        rem = cb - hi
        mid = rem.astype(BF16).astype(F32)
        lo = rem - mid
        q_extra = jnp.where(lane == 64, hi, jnp.where(lane == 65, mid, jnp.where(
            lane == 66, lo, jnp.where(lane < 70, 1.0, 0.0))))
        k_extra = jnp.where(lane < 67, 1.0, jnp.where(lane == 67, -hi, jnp.where(
            lane == 68, -mid, jnp.where(lane == 69, -lo, 0.0))))
        slab = slice(h * LANES, (h + 1) * LANES)

        def head_cols(base):
            blk = qkv[:, base + LANES * (h // 2):base + LANES * (h // 2 + 1)]
            return blk if h % 2 == 0 else pltpu.roll(blk, ATTN_HEAD_DIM, 1)

        q_ref[0, :, slab] = jnp.where(is_data, head_cols(0) * Q_SCALE, q_extra).astype(BF16)
        k_ref[0, :, slab] = jnp.where(is_data, head_cols(ATTN_WIDTH), k_extra).astype(BF16)
        v_ref[0, :, slab] = jnp.where(is_data, head_cols(2 * ATTN_WIDTH), v_extra).astype(BF16)

    r = jnp.dot(xb, wr_ref[...], preferred_element_type=F32)
    qr_ref[0] = _silu(r[:, :REC_WIDTH]).astype(BF16)
    f_r = r[:, REC_WIDTH:2 * REC_WIDTH]
    ls = _log_sigmoid(f_r)
    if layer == 0:
        lfr_ref[0] = ls
        kr_ref[0] = _sigmoid(-f_r).astype(BF16)
    else:
        raw = lbraw_ref[...]
        e = jnp.exp(raw - jnp.max(raw, axis=0, keepdims=True))
        sm = e / jnp.sum(e, axis=0, keepdims=True)
        lb = jnp.sum(sm[1:layer + 1], axis=0, keepdims=True)
        a = jnp.log(lb)
        b = jnp.log(1.0 - lb) + ls
        hi = jnp.maximum(a, b)
        lfr_ref[0] = hi + jnp.log(1.0 + jnp.exp(-jnp.abs(a - b)))
        kr_ref[0] = ((1.0 - lb) * _sigmoid(-f_r)).astype(BF16)
    ir_ref[0] = r[:, 2 * REC_WIDTH:3 * REC_WIDTH].astype(BF16)
    gr_ref[0] = _silu(r[:, 3 * REC_WIDTH:]).astype(BF16)


def _project(x, ln_g, ln_b, wqkv, fbias, wr, lbraw, *, layer):
    B, S, D = x.shape
    tm = PROJ_TOKENS
    apply_ln = ln_g is not None
    depth = lbraw.shape[0]
    slab_width = N_ATTN_HEADS * LANES
    tok = lambda w: pl.BlockSpec((1, tm, w), lambda b, s: (b, s, 0))
    in_specs = [tok(D)]
    args = [x]
    if apply_ln:
        in_specs += [_resident((1, D)), _resident((1, D))]
        args += [ln_g, ln_b]
    in_specs += [_resident(wqkv.shape), _resident(fbias.shape), _resident(wr.shape),
                 _resident((depth, REC_WIDTH))]
    args += [wqkv, fbias, wr, lbraw]
    out_shape, out_specs = [], []
    if apply_ln:
        out_shape.append(jax.ShapeDtypeStruct((B, S, D), F32))
        out_specs.append(tok(D))
    for _ in range(3):
        out_shape.append(jax.ShapeDtypeStruct((B, S, slab_width), BF16))
        out_specs.append(tok(slab_width))
    for _ in range(4):
        out_shape.append(jax.ShapeDtypeStruct((B, S, REC_WIDTH), BF16))
        out_specs.append(tok(REC_WIDTH))
    out_shape.append(jax.ShapeDtypeStruct((B, S, REC_WIDTH), F32))
    out_specs.append(tok(REC_WIDTH))
    outs = pl.pallas_call(
        functools.partial(_proj_kernel, apply_ln=apply_ln, layer=layer, tm=tm),
        grid=(B, S // tm),
        in_specs=in_specs,
        out_specs=out_specs,
        out_shape=out_shape,
        scratch_shapes=[pltpu.VMEM((SUBLANES, LANES), F32)],
        compiler_params=pltpu.CompilerParams(
            dimension_semantics=("arbitrary", "arbitrary"), vmem_limit_bytes=VMEM_LIMIT_BYTES),
        name=f"proj_l{layer}",
    )(*args)
    return outs


def _fox_kernel(q_ref, k_ref, v_ref, g_ref, o_ref, m_ref, acc_ref, sa_ref, sb_ref, *, t):
    qi = pl.program_id(2)
    m_ref[...] = jnp.full_like(m_ref, -jnp.inf)
    acc_ref[...] = jnp.zeros_like(acc_ref)
    q = [q_ref[0, :, h * LANES:(h + 1) * LANES] for h in range(2)]

    def logits(kj, s_ref):
        start = pl.multiple_of(kj * t, t)
        for h in range(2):
            k = k_ref[0, pl.ds(start, t), h * LANES:(h + 1) * LANES]
            s_ref[h] = lax.dot_general(q[h], k, (((1,), (1,)), ((), ())), preferred_element_type=F32)

    def softmax_pv(kj, s_ref, masked):
        start = pl.multiple_of(kj * t, t)
        for h in range(2):
            v = v_ref[0, pl.ds(start, t), h * LANES:(h + 1) * LANES]
            s = s_ref[h]
            if masked:
                r_i = lax.broadcasted_iota(jnp.int32, s.shape, 0)
                c_i = lax.broadcasted_iota(jnp.int32, s.shape, 1)
                s = jnp.where(c_i <= r_i, s, -jnp.inf)
            m_prev = m_ref[h]
            m_next = jnp.maximum(m_prev, jnp.max(s, axis=-1, keepdims=True))
            p = jnp.exp2(s - jnp.tile(m_next, (1, t // LANES)))
            alpha = jnp.exp2(m_prev - m_next)
            acc_ref[h] = alpha * acc_ref[h] + jnp.dot(p.astype(BF16), v, preferred_element_type=F32)
            m_ref[h] = m_next

    logits(0, sa_ref)

    def body(j, carry):
        a = 2 * j
        logits(a + 1, sb_ref)
        softmax_pv(a, sa_ref, False)
        logits(a + 2, sa_ref)
        softmax_pv(a + 1, sb_ref, False)
        return carry

    lax.fori_loop(0, qi // 2, body, 0)

    @pl.when(qi % 2 == 0)
    def _():
        softmax_pv(qi, sa_ref, True)

    @pl.when(qi % 2 == 1)
    def _():
        logits(qi, sb_ref)
        softmax_pv(qi - 1, sa_ref, False)
        softmax_pv(qi, sb_ref, True)

    def normalised(h):
        acc = acc_ref[h]
        return acc / jnp.broadcast_to(acc[:, ATTN_HEAD_DIM:ATTN_HEAD_DIM + 1], acc.shape)

    lane_o = lax.broadcasted_iota(jnp.int32, (t, LANES), 1)
    first_o = lane_o < ATTN_HEAD_DIM
    o = jnp.where(first_o, normalised(0), pltpu.roll(normalised(1), ATTN_HEAD_DIM, 1))
    sq = o * o
    ms0 = jnp.sum(jnp.where(first_o, sq, 0.0), axis=-1, keepdims=True) / ATTN_HEAD_DIM
    ms1 = jnp.sum(jnp.where(first_o, 0.0, sq), axis=-1, keepdims=True) / ATTN_HEAD_DIM
    scale = jnp.where(first_o, lax.rsqrt(ms0 + RMS_EPS), lax.rsqrt(ms1 + RMS_EPS))
    o_ref[0] = (o * scale * g_ref[...]).astype(o_ref.dtype)


def _fox_attention(q, k, v, norm_g, *, layer):
    B, S, _ = q.shape
    t = ATTN_TILE
    pairs = N_ATTN_HEADS // 2
    return pl.pallas_call(
        functools.partial(_fox_kernel, t=t),
        grid=(B, pairs, S // t),
        in_specs=[
            pl.BlockSpec((1, t, 2 * LANES), lambda b, p, i: (b, i, p)),
            pl.BlockSpec((1, S, 2 * LANES), lambda b, p, i: (b, 0, p)),
            pl.BlockSpec((1, S, 2 * LANES), lambda b, p, i: (b, 0, p)),
            pl.BlockSpec((1, LANES), lambda b, p, i: (0, p)),
        ],
        out_specs=pl.BlockSpec((1, t, LANES), lambda b, p, i: (b, i, p)),
        out_shape=jax.ShapeDtypeStruct((B, S, ATTN_WIDTH), BF16),
        scratch_shapes=[pltpu.VMEM((2, t, LANES), F32), pltpu.VMEM((2, t, LANES), F32),
                        pltpu.VMEM((2, t, t), F32), pltpu.VMEM((2, t, t), F32)],
        compiler_params=pltpu.CompilerParams(
            dimension_semantics=("arbitrary", "arbitrary", "arbitrary"),
            vmem_limit_bytes=VMEM_LIMIT_BYTES),
        name=f"fox_l{layer}",
    )(q, k, v, norm_g)


def _reference_rows(b3, level):
    nc, cs, d = b3.shape
    if level >= SUBLANES:
        parts = []
        for g0 in range(0, cs, 2 * level):
            row = b3[:, g0 + level:g0 + level + 1, :]
            parts.append(jnp.broadcast_to(row, (nc, 2 * level, d)))
        return jnp.concatenate(parts, axis=1)
    b4 = b3.reshape(nc * cs // SUBLANES, SUBLANES, d)
    sub = lax.broadcasted_iota(jnp.int32, b4.shape, 1)
    ref = jnp.broadcast_to(b4[:, SUBLANES - level:SUBLANES - level + 1, :], b4.shape)
    for g0 in range(SUBLANES - 4 * level, -1, -2 * level):
        row = jnp.broadcast_to(b4[:, g0 + level:g0 + level + 1, :], b4.shape)
        ref = jnp.where(sub < g0 + 2 * level, row, ref)
    return ref.reshape(nc, cs, d)


def _hgrn_kernel(q_ref, k_ref, v_ref, lf_ref, g_ref, ng_ref, o_ref, st_ref, *, ts):
    cs = REC_CHUNK
    nc = ts // cs
    d = REC_HEAD_DIM

    @pl.when(pl.program_id(1) == 0)
    def _():
        st_ref[...] = jnp.zeros_like(st_ref)

    row2 = lax.broadcasted_iota(jnp.int32, (ts, d), 0) % cs
    row3 = lax.broadcasted_iota(jnp.int32, (nc, cs, d), 1)
    t_i = lax.broadcasted_iota(jnp.int32, (nc, cs, cs), 1)
    s_i = lax.broadcasted_iota(jnp.int32, (nc, cs, cs), 2)

    for h in range(N_REC_HEADS):
        hs = slice(h * d, (h + 1) * d)
        b = lf_ref[0, :, hs]
        shift = 1
        while shift < cs:
            b = b + jnp.where(row2 >= shift, pltpu.roll(b, shift, 0), 0.0)
            shift *= 2
        b3 = b.reshape(nc, cs, d)
        q3 = q_ref[0, :, hs].astype(F32).reshape(nc, cs, d)
        k3 = k_ref[0, :, hs].astype(F32).reshape(nc, cs, d)
        v3b = v_ref[0, :, hs].reshape(nc, cs, d)
        v3 = v3b.astype(F32)
        b_last = jnp.broadcast_to(b3[:, cs - 1:cs, :], b3.shape)

        scores = jnp.zeros((nc, cs, cs), F32)
        level = cs // 2
        while level >= 1:
            ref = _reference_rows(b3, level)
            is_q = (row3 % (2 * level)) >= level
            e = jnp.exp(jnp.where(is_q, b3 - ref, ref - b3))
            xl = (jnp.where(is_q, q3, k3) * e).astype(BF16)
            g = jnp.einsum("ctd,csd->cts", xl, xl, preferred_element_type=F32)
            t_q = (t_i % (2 * level)) >= level
            s_k = (s_i % (2 * level)) < level
            same = (t_i // (2 * level)) == (s_i // (2 * level))
            scores = scores + jnp.where(t_q & s_k & same, g, 0.0)
            level //= 2
        diag = jnp.sum(q3 * k3, axis=-1, keepdims=True)
        o_intra = jnp.einsum("cts,cse->cte", scores.astype(BF16), v3b,
                             preferred_element_type=F32) + diag * v3

        q_dec = (q3 * jnp.exp(b3)).astype(BF16)
        k_dec = (k3 * jnp.exp(b_last - b3)).astype(BF16)

        st = st_ref[h]
        outs = []
        for c in range(nc):
            o_c = o_intra[c] + lax.dot_general(q_dec[c], st.astype(BF16), (((1,), (1,)), ((), ())),
                                               preferred_element_type=F32)
            outs.append(o_c)
            upd = lax.dot_general(v3b[c], k_dec[c], (((0,), (0,)), ((), ())),
                                  preferred_element_type=F32)
            st = st * jnp.exp(b3[c, cs - 1:cs, :]) + upd
        st_ref[h] = st
        o = jnp.concatenate(outs, axis=0)
        ms = jnp.mean(o * o, axis=-1, keepdims=True)
        y = o * lax.rsqrt(ms + RMS_EPS) * ng_ref[:, hs] * g_ref[0, :, hs].astype(F32)
        o_ref[0, :, hs] = y.astype(o_ref.dtype)


def _hgrn(qr, kr, ir, lfr, gr, norm_g, *, layer):
    B, S, W = qr.shape
    ts = REC_TOKENS
    tok = pl.BlockSpec((1, ts, W), lambda b, s: (b, s, 0))
    return pl.pallas_call(
        functools.partial(_hgrn_kernel, ts=ts),
        grid=(B, S // ts),
        in_specs=[tok, tok, tok, tok, tok, _resident((1, W))],
        out_specs=tok,
        out_shape=jax.ShapeDtypeStruct((B, S, W), BF16),
        scratch_shapes=[pltpu.VMEM((N_REC_HEADS, REC_HEAD_DIM, REC_HEAD_DIM), F32)],
        compiler_params=pltpu.CompilerParams(
            dimension_semantics=("arbitrary", "arbitrary"), vmem_limit_bytes=VMEM_LIMIT_BYTES),
        name=f"hgrn_l{layer}",
    )(qr, kr, ir, lfr, gr, norm_g)


def _out_kernel(oa_ref, or_ref, x_ref, wa_ref, wr_ref, g_ref, b_ref, y_ref, *, alpha):
    mix = jnp.dot(oa_ref[...], wa_ref[...], preferred_element_type=F32)
    mix = mix + jnp.dot(or_ref[...], wr_ref[...], preferred_element_type=F32)
    y_ref[...] = _layer_norm(alpha * x_ref[...] + mix, g_ref[...], b_ref[...])


def _out_project(oa, orr, x, wa, wr, g, b, *, alpha, layer):
    N, D = x.shape
    tm = OUT_TOKENS
    return pl.pallas_call(
        functools.partial(_out_kernel, alpha=alpha),
        grid=(N // tm,),
        in_specs=[pl.BlockSpec((tm, ATTN_WIDTH), lambda i: (i, 0)),
                  pl.BlockSpec((tm, REC_WIDTH), lambda i: (i, 0)),
                  pl.BlockSpec((tm, D), lambda i: (i, 0)),
                  _resident(wa.shape), _resident(wr.shape), _resident((1, D)), _resident((1, D))],
        out_specs=pl.BlockSpec((tm, D), lambda i: (i, 0)),
        out_shape=jax.ShapeDtypeStruct((N, D), F32),
        compiler_params=pltpu.CompilerParams(
            dimension_semantics=("arbitrary",), vmem_limit_bytes=VMEM_LIMIT_BYTES),
        name=f"outproj_l{layer}",
    )(oa, orr, x, wa, wr, g, b)


def _ffn_kernel(x_ref, halo_ref, wa_ref, wu_ref, cv_ref, wd_ref, g_ref, b_ref, y_ref, acc_ref,
                *, alpha, tm, tiles_per_seq, n_blocks):
    i = pl.program_id(0)
    x = x_ref[...]
    halo = jnp.where(i % tiles_per_seq == 0, 0.0, halo_ref[...])
    xe = jnp.concatenate([halo, x], axis=0).astype(BF16)
    acc_ref[...] = jnp.zeros_like(acc_ref)

    def conv(h, cw):
        y = cw[CONV_WIDTH:CONV_WIDTH + 1, :]
        for j in range(CONV_WIDTH):
            off = HALO - (CONV_WIDTH - 1) + j
            y = y + cw[j:j + 1, :] * h[off:off + tm, :]
        return y

    def block(j, carry):
        ha = jnp.dot(xe, wa_ref[j], preferred_element_type=F32)
        hu = jnp.dot(xe, wu_ref[j], preferred_element_type=F32)
        a = conv(ha, cv_ref[0, j])
        u = conv(hu, cv_ref[1, j])
        gate = 0.5 * a * (1.0 + lax.erf(a * (2.0 ** -0.5)))
        acc_ref[...] += jnp.dot((gate * u).astype(BF16), wd_ref[j], preferred_element_type=F32)
        return carry

    lax.fori_loop(0, n_blocks, block, 0, unroll=True)
    y_ref[...] = _layer_norm(alpha * x + acc_ref[...], g_ref[...], b_ref[...])


def _ffn(x, wa, wu, cv, wd, g, b, *, alpha, seq, layer):
    N, D = x.shape
    tm = FFN_TOKENS
    n_blocks = wa.shape[0]
    halo_blocks = tm // HALO
    return pl.pallas_call(
        functools.partial(_ffn_kernel, alpha=alpha, tm=tm, tiles_per_seq=seq // tm, n_blocks=n_blocks),
        grid=(N // tm,),
        in_specs=[pl.BlockSpec((tm, D), lambda i: (i, 0)),
                  pl.BlockSpec((HALO, D), lambda i: (jnp.maximum(i * halo_blocks - 1, 0), 0)),
                  _resident(wa.shape), _resident(wu.shape), _resident(cv.shape), _resident(wd.shape),
                  _resident((1, D)), _resident((1, D))],
        out_specs=pl.BlockSpec((tm, D), lambda i: (i, 0)),
        out_shape=jax.ShapeDtypeStruct((N, D), F32),
        scratch_shapes=[pltpu.VMEM((tm, D), F32)],
        compiler_params=pltpu.CompilerParams(
            dimension_semantics=("arbitrary",), vmem_limit_bytes=VMEM_LIMIT_BYTES),
        name=f"ffn_l{layer}",
    )(x, x, wa, wu, cv, wd, g, b)


def _pack_conv(conv_w, conv_b, d_ff):
    rows = jnp.concatenate([conv_w, conv_b[None, :],
                            jnp.zeros((SUBLANES - CONV_WIDTH - 1, conv_w.shape[1]), conv_w.dtype)], axis=0)
    rows = rows.reshape(SUBLANES, 2, d_ff // FFN_BLOCK, FFN_BLOCK)
    return jnp.transpose(rows, (1, 2, 0, 3))


def kernel(x, ln_emb_g, ln_emb_b, w_in, fox_f_bias, fox_norm_g, hgrn_lower_bounds, hgrn_norm_g,
           w_o, ln_mix_g, ln_mix_b, w_up, conv_w, conv_b, w_down, ln_ffn_g, ln_ffn_b):
    B, S, D = x.shape
    depth = w_in.shape[0]
    d_ff = w_down.shape[1]
    alpha = (2.0 * depth) ** 0.25
    n_blocks = d_ff // FFN_BLOCK
    aw, rw = ATTN_WIDTH, REC_WIDTH
    row = lambda a: a.reshape(1, -1)

    xs = x
    for l in range(depth):
        w = w_in[l]
        gate_pad = jnp.zeros((D, LANES - N_ATTN_HEADS), w.dtype)
        wqkv = jnp.concatenate([w[:, :3 * aw + N_ATTN_HEADS], gate_pad], axis=1).astype(BF16)
        wr = w[:, 3 * aw + N_ATTN_HEADS:].astype(BF16)
        fbias = jnp.pad(fox_f_bias[l], (0, LANES - N_ATTN_HEADS)).reshape(1, LANES)
        if l == 0:
            outs = _project(xs, row(ln_emb_g), row(ln_emb_b), wqkv, fbias, wr,
                            hgrn_lower_bounds, layer=l)
            xs, outs = outs[0], outs[1:]
        else:
            outs = _project(xs, None, None, wqkv, fbias, wr, hgrn_lower_bounds, layer=l)
        q, k, v, qr, kr, ir, gr, lfr = outs

        o_a = _fox_attention(q, k, v, row(fox_norm_g[l]), layer=l)
        o_r = _hgrn(qr, kr, ir, lfr, gr, row(hgrn_norm_g[l]), layer=l)

        wo = w_o[l].astype(BF16)
        x2 = xs.reshape(B * S, D)
        x2 = _out_project(o_a.reshape(B * S, aw), o_r.reshape(B * S, rw), x2, wo[:aw], wo[aw:],
                          row(ln_mix_g[l]), row(ln_mix_b[l]), alpha=alpha, layer=l)

        wup = w_up[l].astype(BF16)
        wa = jnp.transpose(wup[:, :d_ff].reshape(D, n_blocks, FFN_BLOCK), (1, 0, 2))
        wu = jnp.transpose(wup[:, d_ff:].reshape(D, n_blocks, FFN_BLOCK), (1, 0, 2))
        wd = w_down[l].astype(BF16).reshape(n_blocks, FFN_BLOCK, D)
        cv = _pack_conv(conv_w[l], conv_b[l], d_ff)
        x2 = _ffn(x2, wa, wu, cv, wd, row(ln_ffn_g[l]), row(ln_ffn_b[l]), alpha=alpha, seq=S, layer=l)
        xs = x2.reshape(B, S, D)
    return xs
```

```python
import functools
import math

import jax
import jax.numpy as jnp
from jax import lax
from jax.experimental import pallas as pl
from jax.experimental.pallas import tpu as pltpu

F32 = jnp.float32
BF16 = jnp.bfloat16

N_ATTN_HEADS = 8
ATTN_HEAD_DIM = 64
ATTN_WIDTH = N_ATTN_HEADS * ATTN_HEAD_DIM
N_REC_HEADS = 4
REC_HEAD_DIM = 128
REC_WIDTH = N_REC_HEADS * REC_HEAD_DIM
CONV_WIDTH = 3
LN_EPS = 1e-5
RMS_EPS = 1e-6
LOG2E = math.log2(math.e)
Q_SCALE = LOG2E * ATTN_HEAD_DIM ** -0.5

LANES = 128
SUBLANES = 8
VMEM_LIMIT_BYTES = 56 * 1024 * 1024

PROJ_TOKENS = 512
ATTN_TILE = 512
REC_TOKENS = 512
REC_CHUNK = 64
OUT_TOKENS = 512
FFN_TOKENS = 512
FFN_BLOCK = 256
FFN_ROWS = FFN_TOKENS
FFN_GROUP = 2
HALO = SUBLANES


def _resident(shape):
    nd = len(shape)
    return pl.BlockSpec(shape, lambda *_: (0,) * nd, pipeline_mode=pl.Buffered(1))


def _layer_norm(x, g, b):
    mu = jnp.mean(x, axis=-1, keepdims=True)
    xc = x - mu
    var = jnp.mean(xc * xc, axis=-1, keepdims=True)
    return xc * lax.rsqrt(var + LN_EPS) * g + b


def _log_sigmoid(z):
    return jnp.minimum(z, 0.0) - jnp.log(1.0 + jnp.exp(-jnp.abs(z)))


def _sigmoid(z):
    return 1.0 / (1.0 + jnp.exp(-z))


def _silu(z):
    return z * _sigmoid(z)


def _proj_kernel(*refs, apply_ln, layer, tm):
    if apply_ln:
        x_ref, lng_ref, lnb_ref = refs[:3]
        refs = refs[3:]
    else:
        x_ref = refs[0]
        refs = refs[1:]
    (wqkv_ref, fbias_ref, wr_ref, lbraw_ref) = refs[:4]
    refs = refs[4:]
    if apply_ln:
        xln_ref = refs[0]
        refs = refs[1:]
    (q_ref, k_ref, v_ref, qr_ref, kr_ref, ir_ref, gr_ref, lfr_ref, carry_ref) = refs

    s_idx = pl.program_id(1)
    x = x_ref[0]
    if apply_ln:
        x = _layer_norm(x, lng_ref[...], lnb_ref[...])
        xln_ref[0] = x
    xb = x.astype(BF16)

    def mm(w_ref, lo, width):
        return jnp.dot(xb, w_ref[:, lo:lo + width], preferred_element_type=F32)

    gate_cols = mm(wqkv_ref, 3 * ATTN_WIDTH, LANES)
    q_cols = mm(wqkv_ref, 0, ATTN_WIDTH)
    logf = _log_sigmoid(gate_cols + fbias_ref[...])
    row = lax.broadcasted_iota(jnp.int32, logf.shape, 0)
    c = logf
    shift = 1
    while shift < tm:
        c = c + jnp.where(row >= shift, pltpu.roll(c, shift, 0), 0.0)
        shift *= 2

    @pl.when(s_idx == 0)
    def _():
        carry_ref[...] = jnp.zeros_like(carry_ref)

    c = c + carry_ref[0:1, :]
    carry_ref[...] = jnp.broadcast_to(c[tm - 1:tm, :], carry_ref.shape)
    c2 = c * LOG2E
    k_cols = mm(wqkv_ref, ATTN_WIDTH, ATTN_WIDTH)
    v_cols = mm(wqkv_ref, 2 * ATTN_WIDTH, ATTN_WIDTH)

    lane = lax.broadcasted_iota(jnp.int32, (tm, LANES), 1)
    is_data = lane < ATTN_HEAD_DIM
    v_extra = jnp.where(lane == ATTN_HEAD_DIM, 1.0, 0.0)

    def head_cols(cols, h):
        blk = cols[:, LANES * (h // 2):LANES * (h // 2 + 1)]
        return blk if h % 2 == 0 else pltpu.roll(blk, ATTN_HEAD_DIM, 1)

    for h in range(N_ATTN_HEADS):
        cb = jnp.broadcast_to(c2[:, h:h + 1], (tm, LANES))
        hi = cb.astype(BF16).astype(F32)
        rem = cb - hi
        mid = rem.astype(BF16).astype(F32)
        lo = rem - mid
        q_extra = jnp.where(lane == 64, hi, jnp.where(lane == 65, mid, jnp.where(
            lane == 66, lo, jnp.where(lane < 70, 1.0, 0.0))))
        k_extra = jnp.where(lane < 67, 1.0, jnp.where(lane == 67, -hi, jnp.where(
            lane == 68, -mid, jnp.where(lane == 69, -lo, 0.0))))
        slab = slice(h * LANES, (h + 1) * LANES)
        q_ref[0, :, slab] = jnp.where(is_data, head_cols(q_cols, h) * Q_SCALE, q_extra).astype(BF16)
        k_ref[0, :, slab] = jnp.where(is_data, head_cols(k_cols, h), k_extra).astype(BF16)

    qr_cols = mm(wr_ref, 0, REC_WIDTH)
    for h in range(N_ATTN_HEADS):
        slab = slice(h * LANES, (h + 1) * LANES)
        v_ref[0, :, slab] = jnp.where(is_data, head_cols(v_cols, h), v_extra).astype(BF16)
    f_r = mm(wr_ref, REC_WIDTH, REC_WIDTH)
    qr_ref[0] = _silu(qr_cols).astype(BF16)
    ir_cols = mm(wr_ref, 2 * REC_WIDTH, REC_WIDTH)
    ls = _log_sigmoid(f_r)
    if layer == 0:
        lfr_ref[0] = ls
        kr_ref[0] = _sigmoid(-f_r).astype(BF16)
    else:
        raw = lbraw_ref[...]
        e = jnp.exp(raw - jnp.max(raw, axis=0, keepdims=True))
        sm = e / jnp.sum(e, axis=0, keepdims=True)
        lb = jnp.sum(sm[1:layer + 1], axis=0, keepdims=True)
        a = jnp.log(lb)
        b = jnp.log(1.0 - lb) + ls
        hi = jnp.maximum(a, b)
        lfr_ref[0] = hi + jnp.log(1.0 + jnp.exp(-jnp.abs(a - b)))
        kr_ref[0] = ((1.0 - lb) * _sigmoid(-f_r)).astype(BF16)
    gr_cols = mm(wr_ref, 3 * REC_WIDTH, REC_WIDTH)
    ir_ref[0] = ir_cols.astype(BF16)
    gr_ref[0] = _silu(gr_cols).astype(BF16)


def _project(x, ln_g, ln_b, wqkv, fbias, wr, lbraw, *, layer):
    B, S, D = x.shape
    tm = PROJ_TOKENS
    apply_ln = ln_g is not None
    depth = lbraw.shape[0]
    slab_width = N_ATTN_HEADS * LANES
    tok = lambda w: pl.BlockSpec((1, tm, w), lambda b, s: (b, s, 0))
    in_specs = [tok(D)]
    args = [x]
    if apply_ln:
        in_specs += [_resident((1, D)), _resident((1, D))]
        args += [ln_g, ln_b]
    in_specs += [_resident(wqkv.shape), _resident(fbias.shape), _resident(wr.shape),
                 _resident((depth, REC_WIDTH))]
    args += [wqkv, fbias, wr, lbraw]
    out_shape, out_specs = [], []
    if apply_ln:
        out_shape.append(jax.ShapeDtypeStruct((B, S, D), F32))
        out_specs.append(tok(D))
    for _ in range(3):
        out_shape.append(jax.ShapeDtypeStruct((B, S, slab_width), BF16))
        out_specs.append(tok(slab_width))
    for _ in range(4):
        out_shape.append(jax.ShapeDtypeStruct((B, S, REC_WIDTH), BF16))
        out_specs.append(tok(REC_WIDTH))
    out_shape.append(jax.ShapeDtypeStruct((B, S, REC_WIDTH), F32))
    out_specs.append(tok(REC_WIDTH))
    outs = pl.pallas_call(
        functools.partial(_proj_kernel, apply_ln=apply_ln, layer=layer, tm=tm),
        grid=(B, S // tm),
        in_specs=in_specs,
        out_specs=out_specs,
        out_shape=out_shape,
        scratch_shapes=[pltpu.VMEM((SUBLANES, LANES), F32)],
        compiler_params=pltpu.CompilerParams(
            dimension_semantics=("arbitrary", "arbitrary"), vmem_limit_bytes=VMEM_LIMIT_BYTES),
        name=f"proj_l{layer}",
    )(*args)
    return outs


def _fox_kernel(q_ref, k_ref, v_ref, g_ref, o_ref, m_ref, acc_ref, sa_ref, sb_ref, *, t):
    qi = pl.program_id(2)
    m_ref[...] = jnp.full_like(m_ref, -jnp.inf)
    acc_ref[...] = jnp.zeros_like(acc_ref)
    q = [q_ref[0, :, h * LANES:(h + 1) * LANES] for h in range(2)]

    def logits(kj, s_ref):
        start = pl.multiple_of(kj * t, t)
        for h in range(2):
            k = k_ref[0, pl.ds(start, t), h * LANES:(h + 1) * LANES]
            s_ref[h] = lax.dot_general(q[h], k, (((1,), (1,)), ((), ())), preferred_element_type=F32)

    def softmax_pv(kj, s_ref, masked):
        start = pl.multiple_of(kj * t, t)
        for h in range(2):
            v = v_ref[0, pl.ds(start, t), h * LANES:(h + 1) * LANES]
            s = s_ref[h]
            if masked:
                r_i = lax.broadcasted_iota(jnp.int32, s.shape, 0)
                c_i = lax.broadcasted_iota(jnp.int32, s.shape, 1)
                s = jnp.where(c_i <= r_i, s, -jnp.inf)
            m_prev = m_ref[h]
            m_next = jnp.maximum(m_prev, jnp.max(s, axis=-1, keepdims=True))
            p = jnp.exp2(s - jnp.tile(m_next, (1, t // LANES)))
            alpha = jnp.exp2(m_prev - m_next)
            acc_ref[h] = alpha * acc_ref[h] + jnp.dot(p.astype(BF16), v, preferred_element_type=F32)
            m_ref[h] = m_next

    logits(0, sa_ref)

    def body(j, carry):
        a = 2 * j
        logits(a + 1, sb_ref)
        softmax_pv(a, sa_ref, False)
        logits(a + 2, sa_ref)
        softmax_pv(a + 1, sb_ref, False)
        return carry

    lax.fori_loop(0, qi // 2, body, 0)

    @pl.when(qi % 2 == 0)
    def _():
        softmax_pv(qi, sa_ref, True)

    @pl.when(qi % 2 == 1)
    def _():
        logits(qi, sb_ref)
        softmax_pv(qi - 1, sa_ref, False)
        softmax_pv(qi, sb_ref, True)

    def normalised(h):
        acc = acc_ref[h]
        return acc / jnp.broadcast_to(acc[:, ATTN_HEAD_DIM:ATTN_HEAD_DIM + 1], acc.shape)

    lane_o = lax.broadcasted_iota(jnp.int32, (t, LANES), 1)
    first_o = lane_o < ATTN_HEAD_DIM
    o = jnp.where(first_o, normalised(0), pltpu.roll(normalised(1), ATTN_HEAD_DIM, 1))
    sq = o * o
    ms0 = jnp.sum(jnp.where(first_o, sq, 0.0), axis=-1, keepdims=True) / ATTN_HEAD_DIM
    ms1 = jnp.sum(jnp.where(first_o, 0.0, sq), axis=-1, keepdims=True) / ATTN_HEAD_DIM
    scale = jnp.where(first_o, lax.rsqrt(ms0 + RMS_EPS), lax.rsqrt(ms1 + RMS_EPS))
    o_ref[0] = (o * scale * g_ref[...]).astype(o_ref.dtype)


def _fox_attention(q, k, v, norm_g, *, layer):
    B, S, _ = q.shape
    t = ATTN_TILE
    pairs = N_ATTN_HEADS // 2
    return pl.pallas_call(
        functools.partial(_fox_kernel, t=t),
        grid=(B, pairs, S // t),
        in_specs=[
            pl.BlockSpec((1, t, 2 * LANES), lambda b, p, i: (b, i, p)),
            pl.BlockSpec((1, S, 2 * LANES), lambda b, p, i: (b, 0, p)),
            pl.BlockSpec((1, S, 2 * LANES), lambda b, p, i: (b, 0, p)),
            pl.BlockSpec((1, LANES), lambda b, p, i: (0, p)),
        ],
        out_specs=pl.BlockSpec((1, t, LANES), lambda b, p, i: (b, i, p)),
        out_shape=jax.ShapeDtypeStruct((B, S, ATTN_WIDTH), BF16),
        scratch_shapes=[pltpu.VMEM((2, t, LANES), F32), pltpu.VMEM((2, t, LANES), F32),
                        pltpu.VMEM((2, t, t), F32), pltpu.VMEM((2, t, t), F32)],
        compiler_params=pltpu.CompilerParams(
            dimension_semantics=("arbitrary", "arbitrary", "arbitrary"),
            vmem_limit_bytes=VMEM_LIMIT_BYTES),
        name=f"fox_l{layer}",
    )(q, k, v, norm_g)


def _reference_rows(b3, level):
    nc, cs, d = b3.shape
    if level >= SUBLANES:
        parts = []
        for g0 in range(0, cs, 2 * level):
            row = b3[:, g0 + level:g0 + level + 1, :]
            parts.append(jnp.broadcast_to(row, (nc, 2 * level, d)))
        return jnp.concatenate(parts, axis=1)
    b4 = b3.reshape(nc * cs // SUBLANES, SUBLANES, d)
    sub = lax.broadcasted_iota(jnp.int32, b4.shape, 1)
    ref = jnp.broadcast_to(b4[:, SUBLANES - level:SUBLANES - level + 1, :], b4.shape)
    for g0 in range(SUBLANES - 4 * level, -1, -2 * level):
        row = jnp.broadcast_to(b4[:, g0 + level:g0 + level + 1, :], b4.shape)
        ref = jnp.where(sub < g0 + 2 * level, row, ref)
    return ref.reshape(nc, cs, d)


def _hgrn_kernel(q_ref, k_ref, v_ref, lf_ref, g_ref, ng_ref, o_ref, st_ref, *, ts):
    cs = REC_CHUNK
    nc = ts // cs
    d = REC_HEAD_DIM

    @pl.when(pl.program_id(1) == 0)
    def _():
        st_ref[...] = jnp.zeros_like(st_ref)

    row2 = lax.broadcasted_iota(jnp.int32, (ts, d), 0) & (cs - 1)
    row3 = lax.broadcasted_iota(jnp.int32, (nc, cs, d), 1)
    t_i = lax.broadcasted_iota(jnp.int32, (nc, cs, cs), 1)
    s_i = lax.broadcasted_iota(jnp.int32, (nc, cs, cs), 2)
    ts_xor = t_i ^ s_i

    for h in range(N_REC_HEADS):
        hs = slice(h * d, (h + 1) * d)
        b = lf_ref[0, :, hs]
        shift = 1
        while shift < cs:
            b = b + jnp.where(row2 >= shift, pltpu.roll(b, shift, 0), 0.0)
            shift *= 2
        b3 = b.reshape(nc, cs, d)
        q3 = q_ref[0, :, hs].astype(F32).reshape(nc, cs, d)
        k3 = k_ref[0, :, hs].astype(F32).reshape(nc, cs, d)
        v3b = v_ref[0, :, hs].reshape(nc, cs, d)
        v3 = v3b.astype(F32)
        b_last = jnp.broadcast_to(b3[:, cs - 1:cs, :], b3.shape)

        scores = jnp.zeros((nc, cs, cs), F32)
        level = cs // 2
        while level >= 1:
            ref = _reference_rows(b3, level)
            is_q = (row3 & level) != 0
            e = jnp.exp(jnp.where(is_q, b3 - ref, ref - b3))
            xl = (jnp.where(is_q, q3, k3) * e).astype(BF16)
            g = jnp.einsum("ctd,csd->cts", xl, xl, preferred_element_type=F32)
            pair = (ts_xor >= level) & (ts_xor < 2 * level) & ((t_i & level) != 0)
            scores = scores + jnp.where(pair, g, 0.0)
            level //= 2
        diag = jnp.sum(q3 * k3, axis=-1, keepdims=True)
        o_intra = jnp.einsum("cts,cse->cte", scores.astype(BF16), v3b,
                             preferred_element_type=F32) + diag * v3

        q_dec = (q3 * jnp.exp(b3)).astype(BF16)
        k_dec = (k3 * jnp.exp(b_last - b3)).astype(BF16)

        st = st_ref[h]
        outs = []
        for c in range(nc):
            o_c = o_intra[c] + lax.dot_general(q_dec[c], st.astype(BF16), (((1,), (1,)), ((), ())),
                                               preferred_element_type=F32)
            outs.append(o_c)
            upd = lax.dot_general(v3b[c], k_dec[c], (((0,), (0,)), ((), ())),
                                  preferred_element_type=F32)
            st = st * jnp.exp(b3[c, cs - 1:cs, :]) + upd
        st_ref[h] = st
        o = jnp.concatenate(outs, axis=0)
        ms = jnp.mean(o * o, axis=-1, keepdims=True)
        y = o * lax.rsqrt(ms + RMS_EPS) * ng_ref[:, hs] * g_ref[0, :, hs].astype(F32)
        o_ref[0, :, hs] = y.astype(o_ref.dtype)


def _hgrn(qr, kr, ir, lfr, gr, norm_g, *, layer):
    B, S, W = qr.shape
    ts = REC_TOKENS
    tok = pl.BlockSpec((1, ts, W), lambda b, s: (b, s, 0))
    return pl.pallas_call(
        functools.partial(_hgrn_kernel, ts=ts),
        grid=(B, S // ts),
        in_specs=[tok, tok, tok, tok, tok, _resident((1, W))],
        out_specs=tok,
        out_shape=jax.ShapeDtypeStruct((B, S, W), BF16),
        scratch_shapes=[pltpu.VMEM((N_REC_HEADS, REC_HEAD_DIM, REC_HEAD_DIM), F32)],
        compiler_params=pltpu.CompilerParams(
            dimension_semantics=("arbitrary", "arbitrary"), vmem_limit_bytes=VMEM_LIMIT_BYTES),
        name=f"hgrn_l{layer}",
    )(qr, kr, ir, lfr, gr, norm_g)


def _out_kernel(oa_ref, or_ref, x_ref, wa_ref, wr_ref, g_ref, b_ref, y_ref, *, alpha):
    mix = jnp.dot(oa_ref[...], wa_ref[...], preferred_element_type=F32)
    mix = mix + jnp.dot(or_ref[...], wr_ref[...], preferred_element_type=F32)
    y_ref[...] = _layer_norm(alpha * x_ref[...] + mix, g_ref[...], b_ref[...])


def _out_project(oa, orr, x, wa, wr, g, b, *, alpha, layer):
    N, D = x.shape
    tm = OUT_TOKENS
    return pl.pallas_call(
        functools.partial(_out_kernel, alpha=alpha),
        grid=(N // tm,),
        in_specs=[pl.BlockSpec((tm, ATTN_WIDTH), lambda i: (i, 0)),
                  pl.BlockSpec((tm, REC_WIDTH), lambda i: (i, 0)),
                  pl.BlockSpec((tm, D), lambda i: (i, 0)),
                  _resident(wa.shape), _resident(wr.shape), _resident((1, D)), _resident((1, D))],
        out_specs=pl.BlockSpec((tm, D), lambda i: (i, 0)),
        out_shape=jax.ShapeDtypeStruct((N, D), F32),
        compiler_params=pltpu.CompilerParams(
            dimension_semantics=("arbitrary",), vmem_limit_bytes=VMEM_LIMIT_BYTES),
        name=f"outproj_l{layer}",
    )(oa, orr, x, wa, wr, g, b)


def _ffn_kernel(x_ref, halo_ref, wa_ref, wu_ref, cv_ref, wd_ref, g_ref, b_ref, y_ref,
                acc_ref, ha_ref, hu_ref, gu_ref, *, alpha, tm, tiles_per_seq, n_blocks):
    i = pl.program_id(0)
    x = x_ref[...]
    halo = jnp.where(i % tiles_per_seq == 0, 0.0, halo_ref[...])
    xe = jnp.concatenate([halo, x], axis=0).astype(BF16)

    def conv(h_ref, slot, r0, rows, cw):
        h = h_ref[slot, pl.ds(r0, rows + HALO), :]
        y = cw[CONV_WIDTH:CONV_WIDTH + 1, :]
        for j in range(CONV_WIDTH):
            off = HALO - (CONV_WIDTH - 1) + j
            y = y + cw[j:j + 1, :] * h[off:off + rows, :]
        return y

    def up(j):
        ha_ref[j % 2] = jnp.dot(xe, wa_ref[j], preferred_element_type=F32)
        hu_ref[j % 2] = jnp.dot(xe, wu_ref[j], preferred_element_type=F32)

    fb = FFN_BLOCK

    def activate(j):
        g = j // FFN_GROUP
        col = (j % FFN_GROUP) * fb
        for r0 in range(0, tm, FFN_ROWS):
            a = conv(ha_ref, j % 2, r0, FFN_ROWS, cv_ref[0, j])
            u = conv(hu_ref, j % 2, r0, FFN_ROWS, cv_ref[1, j])
            gate = 0.5 * a * (1.0 + lax.erf(a * (2.0 ** -0.5)))
            gu_ref[g % 2, pl.ds(r0, FFN_ROWS), col:col + fb] = (gate * u).astype(BF16)

    def down(g):
        j0 = g * FFN_GROUP
        nb = min(FFN_GROUP, n_blocks - j0)
        part = jnp.dot(gu_ref[g % 2, :, :nb * fb], wd_ref[j0 * fb:(j0 + nb) * fb, :],
                       preferred_element_type=F32)
        if g == 0:
            acc_ref[...] = part
        else:
            acc_ref[...] += part

    n_groups = -(-n_blocks // FFN_GROUP)
    up(0)
    for j in range(n_blocks):
        if j + 1 < n_blocks:
            up(j + 1)
        if j % FFN_GROUP == 0 and j > 0:
            down(j // FFN_GROUP - 1)
        activate(j)
    down(n_groups - 1)
    y_ref[...] = _layer_norm(alpha * x + acc_ref[...], g_ref[...], b_ref[...])


def _ffn(x, wa, wu, cv, wd, g, b, *, alpha, seq, layer):
    N, D = x.shape
    tm = FFN_TOKENS
    n_blocks = wa.shape[0]
    halo_blocks = tm // HALO
    return pl.pallas_call(
        functools.partial(_ffn_kernel, alpha=alpha, tm=tm, tiles_per_seq=seq // tm, n_blocks=n_blocks),
        grid=(N // tm,),
        in_specs=[pl.BlockSpec((tm, D), lambda i: (i, 0)),
                  pl.BlockSpec((HALO, D), lambda i: (jnp.maximum(i * halo_blocks - 1, 0), 0)),
                  _resident(wa.shape), _resident(wu.shape), _resident(cv.shape), _resident(wd.shape),
                  _resident((1, D)), _resident((1, D))],
        out_specs=pl.BlockSpec((tm, D), lambda i: (i, 0)),
        out_shape=jax.ShapeDtypeStruct((N, D), F32),
        scratch_shapes=[pltpu.VMEM((tm, D), F32),
                        pltpu.VMEM((2, HALO + tm, FFN_BLOCK), F32),
                        pltpu.VMEM((2, HALO + tm, FFN_BLOCK), F32),
                        pltpu.VMEM((2, tm, FFN_GROUP * FFN_BLOCK), BF16)],
        compiler_params=pltpu.CompilerParams(
            dimension_semantics=("arbitrary",), vmem_limit_bytes=VMEM_LIMIT_BYTES),
        name=f"ffn_l{layer}",
    )(x, x, wa, wu, cv, wd, g, b)


def _pack_conv(conv_w, conv_b, d_ff):
    rows = jnp.concatenate([conv_w, conv_b[None, :],
                            jnp.zeros((SUBLANES - CONV_WIDTH - 1, conv_w.shape[1]), conv_w.dtype)], axis=0)
    rows = rows.reshape(SUBLANES, 2, d_ff // FFN_BLOCK, FFN_BLOCK)
    return jnp.transpose(rows, (1, 2, 0, 3))


def kernel(x, ln_emb_g, ln_emb_b, w_in, fox_f_bias, fox_norm_g, hgrn_lower_bounds, hgrn_norm_g,
           w_o, ln_mix_g, ln_mix_b, w_up, conv_w, conv_b, w_down, ln_ffn_g, ln_ffn_b):
    B, S, D = x.shape
    depth = w_in.shape[0]
    d_ff = w_down.shape[1]
    alpha = (2.0 * depth) ** 0.25
    n_blocks = d_ff // FFN_BLOCK
    aw, rw = ATTN_WIDTH, REC_WIDTH
    row = lambda a: a.reshape(1, -1)

    xs = x
    for l in range(depth):
        w = w_in[l]
        gate_pad = jnp.zeros((D, LANES - N_ATTN_HEADS), w.dtype)
        wqkv = jnp.concatenate([w[:, :3 * aw + N_ATTN_HEADS], gate_pad], axis=1).astype(BF16)
        wr = w[:, 3 * aw + N_ATTN_HEADS:].astype(BF16)
        fbias = jnp.pad(fox_f_bias[l], (0, LANES - N_ATTN_HEADS)).reshape(1, LANES)
        if l == 0:
            outs = _project(xs, row(ln_emb_g), row(ln_emb_b), wqkv, fbias, wr,
                            hgrn_lower_bounds, layer=l)
            xs, outs = outs[0], outs[1:]
        else:
            outs = _project(xs, None, None, wqkv, fbias, wr, hgrn_lower_bounds, layer=l)
        q, k, v, qr, kr, ir, gr, lfr = outs

        o_a = _fox_attention(q, k, v, row(fox_norm_g[l]), layer=l)
        o_r = _hgrn(qr, kr, ir, lfr, gr, row(hgrn_norm_g[l]), layer=l)

        wo = w_o[l].astype(BF16)
        x2 = xs.reshape(B * S, D)
        x2 = _out_project(o_a.reshape(B * S, aw), o_r.reshape(B * S, rw), x2, wo[:aw], wo[aw:],
                          row(ln_mix_g[l]), row(ln_mix_b[l]), alpha=alpha, layer=l)

        wup = w_up[l].astype(BF16)
        wa = jnp.transpose(wup[:, :d_ff].reshape(D, n_blocks, FFN_BLOCK), (1, 0, 2))
        wu = jnp.transpose(wup[:, d_ff:].reshape(D, n_blocks, FFN_BLOCK), (1, 0, 2))
        wd = w_down[l].astype(BF16)
        cv = _pack_conv(conv_w[l], conv_b[l], d_ff)
        x2 = _ffn(x2, wa, wu, cv, wd, row(ln_ffn_g[l]), row(ln_ffn_b[l]), alpha=alpha, seq=S, layer=l)
        xs = x2.reshape(B, S, D)
    return xs
```

```python
import functools
import math

import jax
import jax.numpy as jnp
from jax import lax
from jax.experimental import pallas as pl
from jax.experimental.pallas import tpu as pltpu

F32 = jnp.float32
BF16 = jnp.bfloat16

N_ATTN_HEADS = 8
ATTN_HEAD_DIM = 64
ATTN_WIDTH = N_ATTN_HEADS * ATTN_HEAD_DIM
N_REC_HEADS = 4
REC_HEAD_DIM = 128
REC_WIDTH = N_REC_HEADS * REC_HEAD_DIM
CONV_WIDTH = 3
LN_EPS = 1e-5
RMS_EPS = 1e-6
LOG2E = math.log2(math.e)
Q_SCALE = LOG2E * ATTN_HEAD_DIM ** -0.5

LANES = 128
SUBLANES = 8
VMEM_LIMIT_BYTES = 56 * 1024 * 1024

PROJ_TOKENS = 512
ATTN_TILE = 512
REC_TOKENS = 512
REC_CHUNK = 64
FFN_TOKENS = 512
FFN_BLOCK = 256
FFN_ROWS = FFN_TOKENS
FFN_GROUP = 2
HALO = SUBLANES


def _resident(shape):
    nd = len(shape)
    return pl.BlockSpec(shape, lambda *_: (0,) * nd, pipeline_mode=pl.Buffered(1))


def _layer_norm(x, g, b):
    mu = jnp.mean(x, axis=-1, keepdims=True)
    xc = x - mu
    var = jnp.mean(xc * xc, axis=-1, keepdims=True)
    return xc * lax.rsqrt(var + LN_EPS) * g + b


def _log_sigmoid(z):
    return jnp.minimum(z, 0.0) - jnp.log(1.0 + jnp.exp(-jnp.abs(z)))


def _sigmoid(z):
    return 1.0 / (1.0 + jnp.exp(-z))


def _silu(z):
    return z * _sigmoid(z)


def _proj_kernel(*refs, apply_ln, layer, tm):
    if apply_ln:
        x_ref, lng_ref, lnb_ref = refs[:3]
        refs = refs[3:]
    else:
        x_ref = refs[0]
        refs = refs[1:]
    (wqkv_ref, fbias_ref, wr_ref, lbraw_ref) = refs[:4]
    refs = refs[4:]
    if apply_ln:
        xln_ref = refs[0]
        refs = refs[1:]
    (q_ref, k_ref, v_ref, qr_ref, kr_ref, ir_ref, gr_ref, lfr_ref, carry_ref) = refs

    s_idx = pl.program_id(1)
    x = x_ref[0]
    if apply_ln:
        x = _layer_norm(x, lng_ref[...], lnb_ref[...])
        xln_ref[0] = x
    xb = x.astype(BF16)

    def mm(w_ref, lo, width):
        return jnp.dot(xb, w_ref[:, lo:lo + width], preferred_element_type=F32)

    gate_cols = mm(wqkv_ref, 3 * ATTN_WIDTH, LANES)
    q_cols = mm(wqkv_ref, 0, ATTN_WIDTH)
    logf = _log_sigmoid(gate_cols + fbias_ref[...])
    row = lax.broadcasted_iota(jnp.int32, logf.shape, 0)
    c = logf
    shift = 1
    while shift < tm:
        c = c + jnp.where(row >= shift, pltpu.roll(c, shift, 0), 0.0)
        shift *= 2

    @pl.when(s_idx == 0)
    def _():
        carry_ref[...] = jnp.zeros_like(carry_ref)

    c = c + carry_ref[0:1, :]
    carry_ref[...] = jnp.broadcast_to(c[tm - 1:tm, :], carry_ref.shape)
    c2 = c * LOG2E
    k_cols = mm(wqkv_ref, ATTN_WIDTH, ATTN_WIDTH)
    v_cols = mm(wqkv_ref, 2 * ATTN_WIDTH, ATTN_WIDTH)

    lane = lax.broadcasted_iota(jnp.int32, (tm, LANES), 1)
    is_data = lane < ATTN_HEAD_DIM
    v_extra = jnp.where(lane == ATTN_HEAD_DIM, 1.0, 0.0)

    def head_cols(cols, h):
        blk = cols[:, LANES * (h // 2):LANES * (h // 2 + 1)]
        return blk if h % 2 == 0 else pltpu.roll(blk, ATTN_HEAD_DIM, 1)

    for h in range(N_ATTN_HEADS):
        cb = jnp.broadcast_to(c2[:, h:h + 1], (tm, LANES))
        hi = cb.astype(BF16).astype(F32)
        rem = cb - hi
        mid = rem.astype(BF16).astype(F32)
        lo = rem - mid
        q_extra = jnp.where(lane == 64, hi, jnp.where(lane == 65, mid, jnp.where(
            lane == 66, lo, jnp.where(lane < 70, 1.0, 0.0))))
        k_extra = jnp.where(lane < 67, 1.0, jnp.where(lane == 67, -hi, jnp.where(
            lane == 68, -mid, jnp.where(lane == 69, -lo, 0.0))))
        slab = slice(h * LANES, (h + 1) * LANES)
        q_ref[0, :, slab] = jnp.where(is_data, head_cols(q_cols, h) * Q_SCALE, q_extra).astype(BF16)
        k_ref[0, :, slab] = jnp.where(is_data, head_cols(k_cols, h), k_extra).astype(BF16)

    qr_cols = mm(wr_ref, 0, REC_WIDTH)
    for h in range(N_ATTN_HEADS):
        slab = slice(h * LANES, (h + 1) * LANES)
        v_ref[0, :, slab] = jnp.where(is_data, head_cols(v_cols, h), v_extra).astype(BF16)
    f_r = mm(wr_ref, REC_WIDTH, REC_WIDTH)
    qr_ref[0] = _silu(qr_cols).astype(BF16)
    ir_cols = mm(wr_ref, 2 * REC_WIDTH, REC_WIDTH)
    ls = _log_sigmoid(f_r)
    if layer == 0:
        lfr_ref[0] = ls
        kr_ref[0] = _sigmoid(-f_r).astype(BF16)
    else:
        raw = lbraw_ref[...]
        e = jnp.exp(raw - jnp.max(raw, axis=0, keepdims=True))
        sm = e / jnp.sum(e, axis=0, keepdims=True)
        lb = jnp.sum(sm[1:layer + 1], axis=0, keepdims=True)
        a = jnp.log(lb)
        b = jnp.log(1.0 - lb) + ls
        hi = jnp.maximum(a, b)
        lfr_ref[0] = hi + jnp.log(1.0 + jnp.exp(-jnp.abs(a - b)))
        kr_ref[0] = ((1.0 - lb) * _sigmoid(-f_r)).astype(BF16)
    gr_cols = mm(wr_ref, 3 * REC_WIDTH, REC_WIDTH)
    ir_ref[0] = ir_cols.astype(BF16)
    gr_ref[0] = _silu(gr_cols).astype(BF16)


def _project(x, ln_g, ln_b, wqkv, fbias, wr, lbraw, *, layer):
    B, S, D = x.shape
    tm = PROJ_TOKENS
    apply_ln = ln_g is not None
    depth = lbraw.shape[0]
    slab_width = N_ATTN_HEADS * LANES
    tok = lambda w: pl.BlockSpec((1, tm, w), lambda b, s: (b, s, 0))
    in_specs = [tok(D)]
    args = [x]
    if apply_ln:
        in_specs += [_resident((1, D)), _resident((1, D))]
        args += [ln_g, ln_b]
    in_specs += [_resident(wqkv.shape), _resident(fbias.shape), _resident(wr.shape),
                 _resident((depth, REC_WIDTH))]
    args += [wqkv, fbias, wr, lbraw]
    out_shape, out_specs = [], []
    if apply_ln:
        out_shape.append(jax.ShapeDtypeStruct((B, S, D), F32))
        out_specs.append(tok(D))
    for _ in range(3):
        out_shape.append(jax.ShapeDtypeStruct((B, S, slab_width), BF16))
        out_specs.append(tok(slab_width))
    for _ in range(4):
        out_shape.append(jax.ShapeDtypeStruct((B, S, REC_WIDTH), BF16))
        out_specs.append(tok(REC_WIDTH))
    out_shape.append(jax.ShapeDtypeStruct((B, S, REC_WIDTH), F32))
    out_specs.append(tok(REC_WIDTH))
    outs = pl.pallas_call(
        functools.partial(_proj_kernel, apply_ln=apply_ln, layer=layer, tm=tm),
        grid=(B, S // tm),
        in_specs=in_specs,
        out_specs=out_specs,
        out_shape=out_shape,
        scratch_shapes=[pltpu.VMEM((SUBLANES, LANES), F32)],
        compiler_params=pltpu.CompilerParams(
            dimension_semantics=("arbitrary", "arbitrary"), vmem_limit_bytes=VMEM_LIMIT_BYTES),
        name=f"proj_l{layer}",
    )(*args)
    return outs


def _fox_kernel(q_ref, k_ref, v_ref, g_ref, o_ref, m_ref, acc_ref, sa_ref, sb_ref, *, t):
    qi = pl.program_id(2)
    m_ref[...] = jnp.full_like(m_ref, -jnp.inf)
    acc_ref[...] = jnp.zeros_like(acc_ref)
    q = [q_ref[0, :, h * LANES:(h + 1) * LANES] for h in range(2)]

    def logits(kj, s_ref):
        start = pl.multiple_of(kj * t, t)
        for h in range(2):
            k = k_ref[0, pl.ds(start, t), h * LANES:(h + 1) * LANES]
            s_ref[h] = lax.dot_general(q[h], k, (((1,), (1,)), ((), ())), preferred_element_type=F32)

    def softmax_pv(kj, s_ref, masked):
        start = pl.multiple_of(kj * t, t)
        for h in range(2):
            v = v_ref[0, pl.ds(start, t), h * LANES:(h + 1) * LANES]
            s = s_ref[h]
            if masked:
                r_i = lax.broadcasted_iota(jnp.int32, s.shape, 0)
                c_i = lax.broadcasted_iota(jnp.int32, s.shape, 1)
                s = jnp.where(c_i <= r_i, s, -jnp.inf)
            m_prev = m_ref[h]
            m_next = jnp.maximum(m_prev, jnp.max(s, axis=-1, keepdims=True))
            p = jnp.exp2(s - jnp.tile(m_next, (1, t // LANES)))
            alpha = jnp.exp2(m_prev - m_next)
            acc_ref[h] = alpha * acc_ref[h] + jnp.dot(p.astype(BF16), v, preferred_element_type=F32)
            m_ref[h] = m_next

    logits(0, sa_ref)

    def body(j, carry):
        a = 2 * j
        logits(a + 1, sb_ref)
        softmax_pv(a, sa_ref, False)
        logits(a + 2, sa_ref)
        softmax_pv(a + 1, sb_ref, False)
        return carry

    lax.fori_loop(0, qi // 2, body, 0)

    @pl.when(qi % 2 == 0)
    def _():
        softmax_pv(qi, sa_ref, True)

    @pl.when(qi % 2 == 1)
    def _():
        logits(qi, sb_ref)
        softmax_pv(qi - 1, sa_ref, False)
        softmax_pv(qi, sb_ref, True)

    def normalised(h):
        acc = acc_ref[h]
        return acc / jnp.broadcast_to(acc[:, ATTN_HEAD_DIM:ATTN_HEAD_DIM + 1], acc.shape)

    lane_o = lax.broadcasted_iota(jnp.int32, (t, LANES), 1)
    first_o = lane_o < ATTN_HEAD_DIM
    o = jnp.where(first_o, normalised(0), pltpu.roll(normalised(1), ATTN_HEAD_DIM, 1))
    sq = o * o
    ms0 = jnp.sum(jnp.where(first_o, sq, 0.0), axis=-1, keepdims=True) / ATTN_HEAD_DIM
    ms1 = jnp.sum(jnp.where(first_o, 0.0, sq), axis=-1, keepdims=True) / ATTN_HEAD_DIM
    scale = jnp.where(first_o, lax.rsqrt(ms0 + RMS_EPS), lax.rsqrt(ms1 + RMS_EPS))
    o_ref[0] = (o * scale * g_ref[...]).astype(o_ref.dtype)


def _fox_attention(q, k, v, norm_g, *, layer):
    B, S, _ = q.shape
    t = ATTN_TILE
    pairs = N_ATTN_HEADS // 2
    return pl.pallas_call(
        functools.partial(_fox_kernel, t=t),
        grid=(B, pairs, S // t),
        in_specs=[
            pl.BlockSpec((1, t, 2 * LANES), lambda b, p, i: (b, i, p)),
            pl.BlockSpec((1, S, 2 * LANES), lambda b, p, i: (b, 0, p)),
            pl.BlockSpec((1, S, 2 * LANES), lambda b, p, i: (b, 0, p)),
            pl.BlockSpec((1, LANES), lambda b, p, i: (0, p)),
        ],
        out_specs=pl.BlockSpec((1, t, LANES), lambda b, p, i: (b, i, p)),
        out_shape=jax.ShapeDtypeStruct((B, S, ATTN_WIDTH), BF16),
        scratch_shapes=[pltpu.VMEM((2, t, LANES), F32), pltpu.VMEM((2, t, LANES), F32),
                        pltpu.VMEM((2, t, t), F32), pltpu.VMEM((2, t, t), F32)],
        compiler_params=pltpu.CompilerParams(
            dimension_semantics=("arbitrary", "arbitrary", "arbitrary"),
            vmem_limit_bytes=VMEM_LIMIT_BYTES),
        name=f"fox_l{layer}",
    )(q, k, v, norm_g)


def _reference_rows(b3, level):
    nc, cs, d = b3.shape
    if level >= SUBLANES:
        parts = []
        for g0 in range(0, cs, 2 * level):
            row = b3[:, g0 + level:g0 + level + 1, :]
            parts.append(jnp.broadcast_to(row, (nc, 2 * level, d)))
        return jnp.concatenate(parts, axis=1)
    b4 = b3.reshape(nc * cs // SUBLANES, SUBLANES, d)
    sub = lax.broadcasted_iota(jnp.int32, b4.shape, 1)
    ref = jnp.broadcast_to(b4[:, SUBLANES - level:SUBLANES - level + 1, :], b4.shape)
    for g0 in range(SUBLANES - 4 * level, -1, -2 * level):
        row = jnp.broadcast_to(b4[:, g0 + level:g0 + level + 1, :], b4.shape)
        ref = jnp.where(sub < g0 + 2 * level, row, ref)
    return ref.reshape(nc, cs, d)


def _hgrn_kernel(q_ref, k_ref, v_ref, lf_ref, g_ref, ng_ref, o_ref, st_ref, *, ts):
    cs = REC_CHUNK
    nc = ts // cs
    d = REC_HEAD_DIM

    @pl.when(pl.program_id(1) == 0)
    def _():
        st_ref[...] = jnp.zeros_like(st_ref)

    row2 = lax.broadcasted_iota(jnp.int32, (ts, d), 0) & (cs - 1)
    row3 = lax.broadcasted_iota(jnp.int32, (nc, cs, d), 1)
    t_i = lax.broadcasted_iota(jnp.int32, (nc, cs, cs), 1)
    s_i = lax.broadcasted_iota(jnp.int32, (nc, cs, cs), 2)
    ts_xor = t_i ^ s_i

    for h in range(N_REC_HEADS):
        hs = slice(h * d, (h + 1) * d)
        b = lf_ref[0, :, hs]
        shift = 1
        while shift < cs:
            b = b + jnp.where(row2 >= shift, pltpu.roll(b, shift, 0), 0.0)
            shift *= 2
        b3 = b.reshape(nc, cs, d)
        q3 = q_ref[0, :, hs].astype(F32).reshape(nc, cs, d)
        k3 = k_ref[0, :, hs].astype(F32).reshape(nc, cs, d)
        v3b = v_ref[0, :, hs].reshape(nc, cs, d)
        v3 = v3b.astype(F32)
        b_last = jnp.broadcast_to(b3[:, cs - 1:cs, :], b3.shape)

        scores = jnp.zeros((nc, cs, cs), F32)
        level = cs // 2
        while level >= 1:
            ref = _reference_rows(b3, level)
            is_q = (row3 & level) != 0
            e = jnp.exp(jnp.where(is_q, b3 - ref, ref - b3))
            xl = (jnp.where(is_q, q3, k3) * e).astype(BF16)
            g = jnp.einsum("ctd,csd->cts", xl, xl, preferred_element_type=F32)
            pair = (ts_xor >= level) & (ts_xor < 2 * level) & ((t_i & level) != 0)
            scores = scores + jnp.where(pair, g, 0.0)
            level //= 2
        diag = jnp.sum(q3 * k3, axis=-1, keepdims=True)
        o_intra = jnp.einsum("cts,cse->cte", scores.astype(BF16), v3b,
                             preferred_element_type=F32) + diag * v3

        q_dec = (q3 * jnp.exp(b3)).astype(BF16)
        k_dec = (k3 * jnp.exp(b_last - b3)).astype(BF16)

        st = st_ref[h]
        outs = []
        for c in range(nc):
            o_c = o_intra[c] + lax.dot_general(q_dec[c], st.astype(BF16), (((1,), (1,)), ((), ())),
                                               preferred_element_type=F32)
            outs.append(o_c)
            upd = lax.dot_general(v3b[c], k_dec[c], (((0,), (0,)), ((), ())),
                                  preferred_element_type=F32)
            st = st * jnp.exp(b3[c, cs - 1:cs, :]) + upd
        st_ref[h] = st
        o = jnp.concatenate(outs, axis=0)
        ms = jnp.mean(o * o, axis=-1, keepdims=True)
        y = o * lax.rsqrt(ms + RMS_EPS) * ng_ref[:, hs] * g_ref[0, :, hs].astype(F32)
        o_ref[0, :, hs] = y.astype(o_ref.dtype)


def _hgrn(qr, kr, ir, lfr, gr, norm_g, *, layer):
    B, S, W = qr.shape
    ts = REC_TOKENS
    tok = pl.BlockSpec((1, ts, W), lambda b, s: (b, s, 0))
    return pl.pallas_call(
        functools.partial(_hgrn_kernel, ts=ts),
        grid=(B, S // ts),
        in_specs=[tok, tok, tok, tok, tok, _resident((1, W))],
        out_specs=tok,
        out_shape=jax.ShapeDtypeStruct((B, S, W), BF16),
        scratch_shapes=[pltpu.VMEM((N_REC_HEADS, REC_HEAD_DIM, REC_HEAD_DIM), F32)],
        compiler_params=pltpu.CompilerParams(
            dimension_semantics=("arbitrary", "arbitrary"), vmem_limit_bytes=VMEM_LIMIT_BYTES),
        name=f"hgrn_l{layer}",
    )(qr, kr, ir, lfr, gr, norm_g)


def _ffn_kernel(oa_ref, or_ref, x_ref, woa_ref, wor_ref, mg_ref, mb_ref, wup_ref, cv_ref, wd_ref,
                g_ref, b_ref, y_ref, acc_ref, ha_ref, hu_ref, gu_ref, x1_ref, tail_ref,
                *, alpha, tm, tiles_per_seq, n_blocks):
    i = pl.program_id(0)
    mix = jnp.dot(oa_ref[...], woa_ref[...], preferred_element_type=F32)
    mix = mix + jnp.dot(or_ref[...], wor_ref[...], preferred_element_type=F32)
    x = _layer_norm(alpha * x_ref[...] + mix, mg_ref[...], mb_ref[...])
    x1_ref[...] = x
    halo = jnp.where(i % tiles_per_seq == 0, 0.0, tail_ref[...])
    tail_ref[...] = x[tm - HALO:, :]
    xe = jnp.concatenate([halo, x], axis=0).astype(BF16)
    d_ff = n_blocks * FFN_BLOCK

    def conv(h_ref, slot, r0, rows, cw):
        h = h_ref[slot, pl.ds(r0, rows + HALO), :]
        y = cw[CONV_WIDTH:CONV_WIDTH + 1, :]
        for j in range(CONV_WIDTH):
            off = HALO - (CONV_WIDTH - 1) + j
            y = y + cw[j:j + 1, :] * h[off:off + rows, :]
        return y

    def up(j):
        c0 = j * FFN_BLOCK
        ha_ref[j % 2] = jnp.dot(xe, wup_ref[:, c0:c0 + FFN_BLOCK], preferred_element_type=F32)
        hu_ref[j % 2] = jnp.dot(xe, wup_ref[:, d_ff + c0:d_ff + c0 + FFN_BLOCK],
                                preferred_element_type=F32)

    fb = FFN_BLOCK

    def activate(j):
        g = j // FFN_GROUP
        col = (j % FFN_GROUP) * fb
        for r0 in range(0, tm, FFN_ROWS):
            a = conv(ha_ref, j % 2, r0, FFN_ROWS, cv_ref[0, j])
            u = conv(hu_ref, j % 2, r0, FFN_ROWS, cv_ref[1, j])
            gate = 0.5 * a * (1.0 + lax.erf(a * (2.0 ** -0.5)))
            gu_ref[g % 2, pl.ds(r0, FFN_ROWS), col:col + fb] = (gate * u).astype(BF16)

    def down(g):
        j0 = g * FFN_GROUP
        nb = min(FFN_GROUP, n_blocks - j0)
        part = jnp.dot(gu_ref[g % 2, :, :nb * fb], wd_ref[j0 * fb:(j0 + nb) * fb, :],
                       preferred_element_type=F32)
        if g == 0:
            acc_ref[...] = part
        else:
            acc_ref[...] += part

    n_groups = -(-n_blocks // FFN_GROUP)
    up(0)
    for j in range(n_blocks):
        if j + 1 < n_blocks:
            up(j + 1)
        if j % FFN_GROUP == 0 and j > 0:
            down(j // FFN_GROUP - 1)
        activate(j)
    down(n_groups - 1)
    y_ref[...] = _layer_norm(alpha * x1_ref[...] + acc_ref[...], g_ref[...], b_ref[...])


def _mix_ffn(oa, orr, x, woa, wor, mix_g, mix_b, wup, cv, wd, g, b, *, alpha, seq, layer):
    N, D = x.shape
    tm = FFN_TOKENS
    n_blocks = wd.shape[0] // FFN_BLOCK
    tok = lambda w: pl.BlockSpec((tm, w), lambda i: (i, 0))
    return pl.pallas_call(
        functools.partial(_ffn_kernel, alpha=alpha, tm=tm, tiles_per_seq=seq // tm, n_blocks=n_blocks),
        grid=(N // tm,),
        in_specs=[tok(ATTN_WIDTH), tok(REC_WIDTH), tok(D),
                  _resident(woa.shape), _resident(wor.shape), _resident((1, D)), _resident((1, D)),
                  _resident(wup.shape), _resident(cv.shape), _resident(wd.shape),
                  _resident((1, D)), _resident((1, D))],
        out_specs=tok(D),
        out_shape=jax.ShapeDtypeStruct((N, D), F32),
        scratch_shapes=[pltpu.VMEM((tm, D), F32),
                        pltpu.VMEM((2, HALO + tm, FFN_BLOCK), F32),
                        pltpu.VMEM((2, HALO + tm, FFN_BLOCK), F32),
                        pltpu.VMEM((2, tm, FFN_GROUP * FFN_BLOCK), BF16),
                        pltpu.VMEM((tm, D), F32),
                        pltpu.VMEM((HALO, D), F32)],
        compiler_params=pltpu.CompilerParams(
            dimension_semantics=("arbitrary",), vmem_limit_bytes=VMEM_LIMIT_BYTES),
        name=f"mixffn_l{layer}",
    )(oa, orr, x, woa, wor, mix_g, mix_b, wup, cv, wd, g, b)


def _pack_conv(conv_w, conv_b, d_ff):
    rows = jnp.concatenate([conv_w, conv_b[None, :],
                            jnp.zeros((SUBLANES - CONV_WIDTH - 1, conv_w.shape[1]), conv_w.dtype)], axis=0)
    rows = rows.reshape(SUBLANES, 2, d_ff // FFN_BLOCK, FFN_BLOCK)
    return jnp.transpose(rows, (1, 2, 0, 3))


def kernel(x, ln_emb_g, ln_emb_b, w_in, fox_f_bias, fox_norm_g, hgrn_lower_bounds, hgrn_norm_g,
           w_o, ln_mix_g, ln_mix_b, w_up, conv_w, conv_b, w_down, ln_ffn_g, ln_ffn_b):
    B, S, D = x.shape
    depth = w_in.shape[0]
    d_ff = w_down.shape[1]
    alpha = (2.0 * depth) ** 0.25
    aw, rw = ATTN_WIDTH, REC_WIDTH
    row = lambda a: a.reshape(1, -1)

    xs = x
    for l in range(depth):
        w = w_in[l]
        gate_pad = jnp.zeros((D, LANES - N_ATTN_HEADS), w.dtype)
        wqkv = jnp.concatenate([w[:, :3 * aw + N_ATTN_HEADS], gate_pad], axis=1).astype(BF16)
        wr = w[:, 3 * aw + N_ATTN_HEADS:].astype(BF16)
        fbias = jnp.pad(fox_f_bias[l], (0, LANES - N_ATTN_HEADS)).reshape(1, LANES)
        if l == 0:
            outs = _project(xs, row(ln_emb_g), row(ln_emb_b), wqkv, fbias, wr,
                            hgrn_lower_bounds, layer=l)
            xs, outs = outs[0], outs[1:]
        else:
            outs = _project(xs, None, None, wqkv, fbias, wr, hgrn_lower_bounds, layer=l)
        q, k, v, qr, kr, ir, gr, lfr = outs

        o_a = _fox_attention(q, k, v, row(fox_norm_g[l]), layer=l)
        o_r = _hgrn(qr, kr, ir, lfr, gr, row(hgrn_norm_g[l]), layer=l)

        wo = w_o[l].astype(BF16)
        cv = _pack_conv(conv_w[l], conv_b[l], d_ff)
        x2 = _mix_ffn(o_a.reshape(B * S, aw), o_r.reshape(B * S, rw), xs.reshape(B * S, D),
                      wo[:aw], wo[aw:], row(ln_mix_g[l]), row(ln_mix_b[l]),
                      w_up[l].astype(BF16), cv, w_down[l].astype(BF16),
                      row(ln_ffn_g[l]), row(ln_ffn_b[l]), alpha=alpha, seq=S, layer=l)
        xs = x2.reshape(B, S, D)
    return xs
```

```python
import functools
import math

import jax
import jax.numpy as jnp
from jax import lax
from jax.experimental import pallas as pl
from jax.experimental.pallas import tpu as pltpu

F32 = jnp.float32
BF16 = jnp.bfloat16

N_ATTN_HEADS = 8
ATTN_HEAD_DIM = 64
ATTN_WIDTH = N_ATTN_HEADS * ATTN_HEAD_DIM
N_REC_HEADS = 4
REC_HEAD_DIM = 128
REC_WIDTH = N_REC_HEADS * REC_HEAD_DIM
CONV_WIDTH = 3
LN_EPS = 1e-5
RMS_EPS = 1e-6
LOG2E = math.log2(math.e)
Q_SCALE = LOG2E * ATTN_HEAD_DIM ** -0.5

LANES = 128
SUBLANES = 8
VMEM_LIMIT_BYTES = 56 * 1024 * 1024

PROJ_TOKENS = 512
ATTN_TILE = 512
FOX_HEADS = 4
REC_TOKENS = 512
REC_CHUNK = 64
FFN_TOKENS = 512
FFN_BLOCK = 256
FFN_ROWS = FFN_TOKENS
FFN_GROUP = 2
HALO = SUBLANES


def _resident(shape):
    nd = len(shape)
    return pl.BlockSpec(shape, lambda *_: (0,) * nd, pipeline_mode=pl.Buffered(1))


def _layer_norm(x, g, b):
    mu = jnp.mean(x, axis=-1, keepdims=True)
    xc = x - mu
    var = jnp.mean(xc * xc, axis=-1, keepdims=True)
    return xc * lax.rsqrt(var + LN_EPS) * g + b


def _log_sigmoid(z):
    return jnp.minimum(z, 0.0) - jnp.log(1.0 + jnp.exp(-jnp.abs(z)))


def _sigmoid(z):
    return 1.0 / (1.0 + jnp.exp(-z))


def _silu(z):
    return z * _sigmoid(z)


def _proj_kernel(*refs, apply_ln, layer, tm):
    if apply_ln:
        x_ref, lng_ref, lnb_ref = refs[:3]
        refs = refs[3:]
    else:
        x_ref = refs[0]
        refs = refs[1:]
    (wqkv_ref, fbias_ref, wr_ref, lbraw_ref) = refs[:4]
    refs = refs[4:]
    if apply_ln:
        xln_ref = refs[0]
        refs = refs[1:]
    (q_ref, k_ref, v_ref, qr_ref, kr_ref, ir_ref, gr_ref, lfr_ref, carry_ref) = refs

    s_idx = pl.program_id(1)
    x = x_ref[0]
    if apply_ln:
        x = _layer_norm(x, lng_ref[...], lnb_ref[...])
        xln_ref[0] = x
    xb = x.astype(BF16)

    def mm(w_ref, lo, width):
        return jnp.dot(xb, w_ref[:, lo:lo + width], preferred_element_type=F32)

    gate_cols = mm(wqkv_ref, 3 * ATTN_WIDTH, LANES)
    q_cols = mm(wqkv_ref, 0, ATTN_WIDTH)
    logf = _log_sigmoid(gate_cols + fbias_ref[...])
    row = lax.broadcasted_iota(jnp.int32, logf.shape, 0)
    c = logf
    shift = 1
    while shift < tm:
        c = c + jnp.where(row >= shift, pltpu.roll(c, shift, 0), 0.0)
        shift *= 2

    @pl.when(s_idx == 0)
    def _():
        carry_ref[...] = jnp.zeros_like(carry_ref)

    c = c + carry_ref[0:1, :]
    carry_ref[...] = jnp.broadcast_to(c[tm - 1:tm, :], carry_ref.shape)
    c2 = c * LOG2E
    k_cols = mm(wqkv_ref, ATTN_WIDTH, ATTN_WIDTH)
    v_cols = mm(wqkv_ref, 2 * ATTN_WIDTH, ATTN_WIDTH)

    lane = lax.broadcasted_iota(jnp.int32, (tm, LANES), 1)
    is_data = lane < ATTN_HEAD_DIM
    v_extra = jnp.where(lane == ATTN_HEAD_DIM, 1.0, 0.0)

    def head_cols(cols, h):
        blk = cols[:, LANES * (h // 2):LANES * (h // 2 + 1)]
        return blk if h % 2 == 0 else pltpu.roll(blk, ATTN_HEAD_DIM, 1)

    for h in range(N_ATTN_HEADS):
        cb = jnp.broadcast_to(c2[:, h:h + 1], (tm, LANES))
        hi = cb.astype(BF16).astype(F32)
        rem = cb - hi
        mid = rem.astype(BF16).astype(F32)
        lo = rem - mid
        q_extra = jnp.where(lane == 64, hi, jnp.where(lane == 65, mid, jnp.where(
            lane == 66, lo, jnp.where(lane < 70, 1.0, 0.0))))
        k_extra = jnp.where(lane < 67, 1.0, jnp.where(lane == 67, -hi, jnp.where(
            lane == 68, -mid, jnp.where(lane == 69, -lo, 0.0))))
        slab = slice(h * LANES, (h + 1) * LANES)
        q_ref[0, :, slab] = jnp.where(is_data, head_cols(q_cols, h) * Q_SCALE, q_extra).astype(BF16)
        k_ref[0, :, slab] = jnp.where(is_data, head_cols(k_cols, h), k_extra).astype(BF16)

    qr_cols = mm(wr_ref, 0, REC_WIDTH)
    for h in range(N_ATTN_HEADS):
        slab = slice(h * LANES, (h + 1) * LANES)
        v_ref[0, :, slab] = jnp.where(is_data, head_cols(v_cols, h), v_extra).astype(BF16)
    f_r = mm(wr_ref, REC_WIDTH, REC_WIDTH)
    qr_ref[0] = _silu(qr_cols).astype(BF16)
    ir_cols = mm(wr_ref, 2 * REC_WIDTH, REC_WIDTH)
    ls = _log_sigmoid(f_r)
    if layer == 0:
        lfr_ref[0] = ls
        kr_ref[0] = _sigmoid(-f_r).astype(BF16)
    else:
        raw = lbraw_ref[...]
        e = jnp.exp(raw - jnp.max(raw, axis=0, keepdims=True))
        sm = e / jnp.sum(e, axis=0, keepdims=True)
        lb = jnp.sum(sm[1:layer + 1], axis=0, keepdims=True)
        a = jnp.log(lb)
        b = jnp.log(1.0 - lb) + ls
        hi = jnp.maximum(a, b)
        lfr_ref[0] = hi + jnp.log(1.0 + jnp.exp(-jnp.abs(a - b)))
        kr_ref[0] = ((1.0 - lb) * _sigmoid(-f_r)).astype(BF16)
    gr_cols = mm(wr_ref, 3 * REC_WIDTH, REC_WIDTH)
    ir_ref[0] = ir_cols.astype(BF16)
    gr_ref[0] = _silu(gr_cols).astype(BF16)


def _project(x, ln_g, ln_b, wqkv, fbias, wr, lbraw, *, layer):
    B, S, D = x.shape
    tm = PROJ_TOKENS
    apply_ln = ln_g is not None
    depth = lbraw.shape[0]
    slab_width = N_ATTN_HEADS * LANES
    tok = lambda w: pl.BlockSpec((1, tm, w), lambda b, s: (b, s, 0))
    in_specs = [tok(D)]
    args = [x]
    if apply_ln:
        in_specs += [_resident((1, D)), _resident((1, D))]
        args += [ln_g, ln_b]
    in_specs += [_resident(wqkv.shape), _resident(fbias.shape), _resident(wr.shape),
                 _resident((depth, REC_WIDTH))]
    args += [wqkv, fbias, wr, lbraw]
    out_shape, out_specs = [], []
    if apply_ln:
        out_shape.append(jax.ShapeDtypeStruct((B, S, D), F32))
        out_specs.append(tok(D))
    for _ in range(3):
        out_shape.append(jax.ShapeDtypeStruct((B, S, slab_width), BF16))
        out_specs.append(tok(slab_width))
    for _ in range(4):
        out_shape.append(jax.ShapeDtypeStruct((B, S, REC_WIDTH), BF16))
        out_specs.append(tok(REC_WIDTH))
    out_shape.append(jax.ShapeDtypeStruct((B, S, REC_WIDTH), F32))
    out_specs.append(tok(REC_WIDTH))
    outs = pl.pallas_call(
        functools.partial(_proj_kernel, apply_ln=apply_ln, layer=layer, tm=tm),
        grid=(B, S // tm),
        in_specs=in_specs,
        out_specs=out_specs,
        out_shape=out_shape,
        scratch_shapes=[pltpu.VMEM((SUBLANES, LANES), F32)],
        compiler_params=pltpu.CompilerParams(
            dimension_semantics=("arbitrary", "arbitrary"), vmem_limit_bytes=VMEM_LIMIT_BYTES),
        name=f"proj_l{layer}",
    )(*args)
    return outs


def _fox_kernel(q_ref, k_ref, v_ref, g_ref, o_ref, m_ref, acc_ref, sa_ref, sb_ref, *, t):
    qi = pl.program_id(2)
    m_ref[...] = jnp.full_like(m_ref, -jnp.inf)
    acc_ref[...] = jnp.zeros_like(acc_ref)
    q = [q_ref[0, :, h * LANES:(h + 1) * LANES] for h in range(FOX_HEADS)]

    def logits(kj, s_ref):
        start = pl.multiple_of(kj * t, t)
        for h in range(FOX_HEADS):
            k = k_ref[0, pl.ds(start, t), h * LANES:(h + 1) * LANES]
            s_ref[h] = lax.dot_general(q[h], k, (((1,), (1,)), ((), ())), preferred_element_type=F32)

    def softmax_pv(kj, s_ref, masked):
        start = pl.multiple_of(kj * t, t)
        for h in range(FOX_HEADS):
            v = v_ref[0, pl.ds(start, t), h * LANES:(h + 1) * LANES]
            s = s_ref[h]
            if masked:
                r_i = lax.broadcasted_iota(jnp.int32, s.shape, 0)
                c_i = lax.broadcasted_iota(jnp.int32, s.shape, 1)
                s = jnp.where(c_i <= r_i, s, -jnp.inf)
            m_prev = m_ref[h]
            m_next = jnp.maximum(m_prev, jnp.max(s, axis=-1, keepdims=True))
            p = jnp.exp2(s - jnp.tile(m_next, (1, t // LANES)))
            alpha = jnp.exp2(m_prev - m_next)
            acc_ref[h] = alpha * acc_ref[h] + jnp.dot(p.astype(BF16), v, preferred_element_type=F32)
            m_ref[h] = m_next

    logits(0, sa_ref)

    def body(j, carry):
        a = 2 * j
        logits(a + 1, sb_ref)
        softmax_pv(a, sa_ref, False)
        logits(a + 2, sa_ref)
        softmax_pv(a + 1, sb_ref, False)
        return carry

    lax.fori_loop(0, qi // 2, body, 0)

    @pl.when(qi % 2 == 0)
    def _():
        softmax_pv(qi, sa_ref, True)

    @pl.when(qi % 2 == 1)
    def _():
        logits(qi, sb_ref)
        softmax_pv(qi - 1, sa_ref, False)
        softmax_pv(qi, sb_ref, True)

    def normalised(h):
        acc = acc_ref[h]
        return acc / jnp.broadcast_to(acc[:, ATTN_HEAD_DIM:ATTN_HEAD_DIM + 1], acc.shape)

    lane_o = lax.broadcasted_iota(jnp.int32, (t, LANES), 1)
    first_o = lane_o < ATTN_HEAD_DIM
    for p in range(FOX_HEADS // 2):
        o = jnp.where(first_o, normalised(2 * p), pltpu.roll(normalised(2 * p + 1), ATTN_HEAD_DIM, 1))
        sq = o * o
        ms0 = jnp.sum(jnp.where(first_o, sq, 0.0), axis=-1, keepdims=True) / ATTN_HEAD_DIM
        ms1 = jnp.sum(jnp.where(first_o, 0.0, sq), axis=-1, keepdims=True) / ATTN_HEAD_DIM
        scale = jnp.where(first_o, lax.rsqrt(ms0 + RMS_EPS), lax.rsqrt(ms1 + RMS_EPS))
        cols = slice(p * LANES, (p + 1) * LANES)
        o_ref[0, :, cols] = (o * scale * g_ref[:, cols]).astype(o_ref.dtype)


def _fox_attention(q, k, v, norm_g, *, layer):
    B, S, _ = q.shape
    t = ATTN_TILE
    groups = N_ATTN_HEADS // FOX_HEADS
    wq, wo = FOX_HEADS * LANES, FOX_HEADS // 2 * LANES
    return pl.pallas_call(
        functools.partial(_fox_kernel, t=t),
        grid=(B, groups, S // t),
        in_specs=[
            pl.BlockSpec((1, t, wq), lambda b, p, i: (b, i, p)),
            pl.BlockSpec((1, S, wq), lambda b, p, i: (b, 0, p)),
            pl.BlockSpec((1, S, wq), lambda b, p, i: (b, 0, p)),
            pl.BlockSpec((1, wo), lambda b, p, i: (0, p)),
        ],
        out_specs=pl.BlockSpec((1, t, wo), lambda b, p, i: (b, i, p)),
        out_shape=jax.ShapeDtypeStruct((B, S, ATTN_WIDTH), BF16),
        scratch_shapes=[pltpu.VMEM((FOX_HEADS, t, LANES), F32), pltpu.VMEM((FOX_HEADS, t, LANES), F32),
                        pltpu.VMEM((FOX_HEADS, t, t), F32), pltpu.VMEM((FOX_HEADS, t, t), F32)],
        compiler_params=pltpu.CompilerParams(
            dimension_semantics=("arbitrary", "arbitrary", "arbitrary"),
            vmem_limit_bytes=VMEM_LIMIT_BYTES),
        name=f"fox_l{layer}",
    )(q, k, v, norm_g)


def _reference_rows(b3, level):
    nc, cs, d = b3.shape
    if level >= SUBLANES:
        parts = []
        for g0 in range(0, cs, 2 * level):
            row = b3[:, g0 + level:g0 + level + 1, :]
            parts.append(jnp.broadcast_to(row, (nc, 2 * level, d)))
        return jnp.concatenate(parts, axis=1)
    b4 = b3.reshape(nc * cs // SUBLANES, SUBLANES, d)
    sub = lax.broadcasted_iota(jnp.int32, b4.shape, 1)
    ref = jnp.broadcast_to(b4[:, SUBLANES - level:SUBLANES - level + 1, :], b4.shape)
    for g0 in range(SUBLANES - 4 * level, -1, -2 * level):
        row = jnp.broadcast_to(b4[:, g0 + level:g0 + level + 1, :], b4.shape)
        ref = jnp.where(sub < g0 + 2 * level, row, ref)
    return ref.reshape(nc, cs, d)


def _hgrn_kernel(q_ref, k_ref, v_ref, lf_ref, g_ref, ng_ref, o_ref, st_ref, *, ts):
    cs = REC_CHUNK
    nc = ts // cs
    d = REC_HEAD_DIM

    @pl.when(pl.program_id(1) == 0)
    def _():
        st_ref[...] = jnp.zeros_like(st_ref)

    row2 = lax.broadcasted_iota(jnp.int32, (ts, d), 0) & (cs - 1)
    row3 = lax.broadcasted_iota(jnp.int32, (nc, cs, d), 1)
    t_i = lax.broadcasted_iota(jnp.int32, (nc, cs, cs), 1)
    s_i = lax.broadcasted_iota(jnp.int32, (nc, cs, cs), 2)
    ts_xor = t_i ^ s_i

    for h in range(N_REC_HEADS):
        hs = slice(h * d, (h + 1) * d)
        b = lf_ref[0, :, hs]
        shift = 1
        while shift < cs:
            b = b + jnp.where(row2 >= shift, pltpu.roll(b, shift, 0), 0.0)
            shift *= 2
        b3 = b.reshape(nc, cs, d)
        q3 = q_ref[0, :, hs].astype(F32).reshape(nc, cs, d)
        k3 = k_ref[0, :, hs].astype(F32).reshape(nc, cs, d)
        v3b = v_ref[0, :, hs].reshape(nc, cs, d)
        v3 = v3b.astype(F32)
        b_last = jnp.broadcast_to(b3[:, cs - 1:cs, :], b3.shape)

        scores = jnp.zeros((nc, cs, cs), F32)
        level = cs // 2
        while level >= 1:
            ref = _reference_rows(b3, level)
            is_q = (row3 & level) != 0
            e = jnp.exp(jnp.where(is_q, b3 - ref, ref - b3))
            xl = (jnp.where(is_q, q3, k3) * e).astype(BF16)
            g = jnp.einsum("ctd,csd->cts", xl, xl, preferred_element_type=F32)
            pair = (ts_xor >= level) & (ts_xor < 2 * level) & ((t_i & level) != 0)
            scores = scores + jnp.where(pair, g, 0.0)
            level //= 2
        diag = jnp.sum(q3 * k3, axis=-1, keepdims=True)
        o_intra = jnp.einsum("cts,cse->cte", scores.astype(BF16), v3b,
                             preferred_element_type=F32) + diag * v3

        q_dec = (q3 * jnp.exp(b3)).astype(BF16)
        k_dec = (k3 * jnp.exp(b_last - b3)).astype(BF16)

        st = st_ref[h]
        outs = []
        for c in range(nc):
            o_c = o_intra[c] + lax.dot_general(q_dec[c], st.astype(BF16), (((1,), (1,)), ((), ())),
                                               preferred_element_type=F32)
            outs.append(o_c)
            upd = lax.dot_general(v3b[c], k_dec[c], (((0,), (0,)), ((), ())),
                                  preferred_element_type=F32)
            st = st * jnp.exp(b3[c, cs - 1:cs, :]) + upd
        st_ref[h] = st
        o = jnp.concatenate(outs, axis=0)
        ms = jnp.mean(o * o, axis=-1, keepdims=True)
        y = o * lax.rsqrt(ms + RMS_EPS) * ng_ref[:, hs] * g_ref[0, :, hs].astype(F32)
        o_ref[0, :, hs] = y.astype(o_ref.dtype)


def _hgrn(qr, kr, ir, lfr, gr, norm_g, *, layer):
    B, S, W = qr.shape
    ts = REC_TOKENS
    tok = pl.BlockSpec((1, ts, W), lambda b, s: (b, s, 0))
    return pl.pallas_call(
        functools.partial(_hgrn_kernel, ts=ts),
        grid=(B, S // ts),
        in_specs=[tok, tok, tok, tok, tok, _resident((1, W))],
        out_specs=tok,
        out_shape=jax.ShapeDtypeStruct((B, S, W), BF16),
        scratch_shapes=[pltpu.VMEM((N_REC_HEADS, REC_HEAD_DIM, REC_HEAD_DIM), F32)],
        compiler_params=pltpu.CompilerParams(
            dimension_semantics=("arbitrary", "arbitrary"), vmem_limit_bytes=VMEM_LIMIT_BYTES),
        name=f"hgrn_l{layer}",
    )(qr, kr, ir, lfr, gr, norm_g)


def _ffn_kernel(oa_ref, or_ref, x_ref, woa_ref, wor_ref, mg_ref, mb_ref, wup_ref, cv_ref, wd_ref,
                g_ref, b_ref, y_ref, acc_ref, ha_ref, hu_ref, gu_ref, x1_ref, tail_ref,
                *, alpha, tm, tiles_per_seq, n_blocks):
    i = pl.program_id(0)
    mix = jnp.dot(oa_ref[...], woa_ref[...], preferred_element_type=F32)
    mix = mix + jnp.dot(or_ref[...], wor_ref[...], preferred_element_type=F32)
    x = _layer_norm(alpha * x_ref[...] + mix, mg_ref[...], mb_ref[...])
    x1_ref[...] = x
    halo = jnp.where(i % tiles_per_seq == 0, 0.0, tail_ref[...])
    tail_ref[...] = x[tm - HALO:, :]
    xe = jnp.concatenate([halo, x], axis=0).astype(BF16)
    d_ff = n_blocks * FFN_BLOCK

    def conv(h_ref, slot, r0, rows, cw):
        h = h_ref[slot, pl.ds(r0, rows + HALO), :]
        y = cw[CONV_WIDTH:CONV_WIDTH + 1, :]
        for j in range(CONV_WIDTH):
            off = HALO - (CONV_WIDTH - 1) + j
            y = y + cw[j:j + 1, :] * h[off:off + rows, :]
        return y

    def up(j):
        c0 = j * FFN_BLOCK
        ha_ref[j % 2] = jnp.dot(xe, wup_ref[:, c0:c0 + FFN_BLOCK], preferred_element_type=F32)
        hu_ref[j % 2] = jnp.dot(xe, wup_ref[:, d_ff + c0:d_ff + c0 + FFN_BLOCK],
                                preferred_element_type=F32)

    fb = FFN_BLOCK

    def activate(j):
        g = j // FFN_GROUP
        col = (j % FFN_GROUP) * fb
        for r0 in range(0, tm, FFN_ROWS):
            a = conv(ha_ref, j % 2, r0, FFN_ROWS, cv_ref[0, j])
            u = conv(hu_ref, j % 2, r0, FFN_ROWS, cv_ref[1, j])
            gate = 0.5 * a * (1.0 + lax.erf(a * (2.0 ** -0.5)))
            gu_ref[g % 2, pl.ds(r0, FFN_ROWS), col:col + fb] = (gate * u).astype(BF16)

    def down(g):
        j0 = g * FFN_GROUP
        nb = min(FFN_GROUP, n_blocks - j0)
        part = jnp.dot(gu_ref[g % 2, :, :nb * fb], wd_ref[j0 * fb:(j0 + nb) * fb, :],
                       preferred_element_type=F32)
        if g == 0:
            acc_ref[...] = part
        else:
            acc_ref[...] += part

    n_groups = -(-n_blocks // FFN_GROUP)
    up(0)
    for j in range(n_blocks):
        if j + 1 < n_blocks:
            up(j + 1)
        if j % FFN_GROUP == 0 and j > 0:
            down(j // FFN_GROUP - 1)
        activate(j)
    down(n_groups - 1)
    y_ref[...] = _layer_norm(alpha * x1_ref[...] + acc_ref[...], g_ref[...], b_ref[...])


def _mix_ffn(oa, orr, x, woa, wor, mix_g, mix_b, wup, cv, wd, g, b, *, alpha, seq, layer):
    N, D = x.shape
    tm = FFN_TOKENS
    n_blocks = wd.shape[0] // FFN_BLOCK
    tok = lambda w: pl.BlockSpec((tm, w), lambda i: (i, 0))
    return pl.pallas_call(
        functools.partial(_ffn_kernel, alpha=alpha, tm=tm, tiles_per_seq=seq // tm, n_blocks=n_blocks),
        grid=(N // tm,),
        in_specs=[tok(ATTN_WIDTH), tok(REC_WIDTH), tok(D),
                  _resident(woa.shape), _resident(wor.shape), _resident((1, D)), _resident((1, D)),
                  _resident(wup.shape), _resident(cv.shape), _resident(wd.shape),
                  _resident((1, D)), _resident((1, D))],
        out_specs=tok(D),
        out_shape=jax.ShapeDtypeStruct((N, D), F32),
        scratch_shapes=[pltpu.VMEM((tm, D), F32),
                        pltpu.VMEM((2, HALO + tm, FFN_BLOCK), F32),
                        pltpu.VMEM((2, HALO + tm, FFN_BLOCK), F32),
                        pltpu.VMEM((2, tm, FFN_GROUP * FFN_BLOCK), BF16),
                        pltpu.VMEM((tm, D), F32),
                        pltpu.VMEM((HALO, D), F32)],
        compiler_params=pltpu.CompilerParams(
            dimension_semantics=("arbitrary",), vmem_limit_bytes=VMEM_LIMIT_BYTES),
        name=f"mixffn_l{layer}",
    )(oa, orr, x, woa, wor, mix_g, mix_b, wup, cv, wd, g, b)


def _pack_conv(conv_w, conv_b, d_ff):
    rows = jnp.concatenate([conv_w, conv_b[None, :],
                            jnp.zeros((SUBLANES - CONV_WIDTH - 1, conv_w.shape[1]), conv_w.dtype)], axis=0)
    rows = rows.reshape(SUBLANES, 2, d_ff // FFN_BLOCK, FFN_BLOCK)
    return jnp.transpose(rows, (1, 2, 0, 3))


def kernel(x, ln_emb_g, ln_emb_b, w_in, fox_f_bias, fox_norm_g, hgrn_lower_bounds, hgrn_norm_g,
           w_o, ln_mix_g, ln_mix_b, w_up, conv_w, conv_b, w_down, ln_ffn_g, ln_ffn_b):
    B, S, D = x.shape
    depth = w_in.shape[0]
    d_ff = w_down.shape[1]
    alpha = (2.0 * depth) ** 0.25
    aw, rw = ATTN_WIDTH, REC_WIDTH
    row = lambda a: a.reshape(1, -1)

    xs = x
    for l in range(depth):
        w = w_in[l]
        gate_pad = jnp.zeros((D, LANES - N_ATTN_HEADS), w.dtype)
        wqkv = jnp.concatenate([w[:, :3 * aw + N_ATTN_HEADS], gate_pad], axis=1).astype(BF16)
        wr = w[:, 3 * aw + N_ATTN_HEADS:].astype(BF16)
        fbias = jnp.pad(fox_f_bias[l], (0, LANES - N_ATTN_HEADS)).reshape(1, LANES)
        if l == 0:
            outs = _project(xs, row(ln_emb_g), row(ln_emb_b), wqkv, fbias, wr,
                            hgrn_lower_bounds, layer=l)
            xs, outs = outs[0], outs[1:]
        else:
            outs = _project(xs, None, None, wqkv, fbias, wr, hgrn_lower_bounds, layer=l)
        q, k, v, qr, kr, ir, gr, lfr = outs

        o_a = _fox_attention(q, k, v, row(fox_norm_g[l]), layer=l)
        o_r = _hgrn(qr, kr, ir, lfr, gr, row(hgrn_norm_g[l]), layer=l)

        wo = w_o[l].astype(BF16)
        cv = _pack_conv(conv_w[l], conv_b[l], d_ff)
        x2 = _mix_ffn(o_a.reshape(B * S, aw), o_r.reshape(B * S, rw), xs.reshape(B * S, D),
                      wo[:aw], wo[aw:], row(ln_mix_g[l]), row(ln_mix_b[l]),
                      w_up[l].astype(BF16), cv, w_down[l].astype(BF16),
                      row(ln_ffn_g[l]), row(ln_ffn_b[l]), alpha=alpha, seq=S, layer=l)
        xs = x2.reshape(B, S, D)
    return xs
```
